```python
import math
import jax, jax.numpy as jnp
from jax import lax
import numpy as np

D_MODEL = 1024
BATCH = 8
SEQ = 4096
DEPTH = 4

A_HEADS = 8
A_HEAD_DIM = 64
A_WIDTH = A_HEADS * A_HEAD_DIM
DILATED_PATTERNS = ((128, 1), (512, 4), (2048, 16))
MLA_HEADS = 4
QK_NOPE = 128
QK_ROPE = 64
V_DIM = 128
Q_LORA = 256
KV_LORA = 128
MLA_WIDTH = MLA_HEADS * V_DIM
MIX_WIDTH = A_WIDTH + MLA_WIDTH
IN_COLS = 3 * A_WIDTH + Q_LORA + KV_LORA + QK_ROPE
Q_BLOCK = 128
D_FF = 4 * D_MODEL
ROPE_THETA = 10000.0
ALPHA = (2.0 * DEPTH) ** 0.25
BETA_INIT = (8.0 * DEPTH) ** -0.25
LN_EPS = 1e-5
RMS_EPS = 1e-6

kernel_name = 'hybrid_dilated_mla_deepnorm'


def _layer_norm(x, g, b):
    x32 = x.astype(jnp.float32)
    mu = jnp.mean(x32, -1, keepdims=True)
    var = jnp.mean(jnp.square(x32 - mu), -1, keepdims=True)
    y = (x32 - mu) * lax.rsqrt(var + LN_EPS)
    return (y * g.astype(jnp.float32) + b.astype(jnp.float32)).astype(x.dtype)


def _rms_norm(x, g):
    x32 = x.astype(jnp.float32)
    y = x32 * lax.rsqrt(jnp.mean(jnp.square(x32), -1, keepdims=True) + RMS_EPS)
    return (y * g.astype(jnp.float32)).astype(x.dtype)


def _rope(x, pos):
    half = x.shape[-1] // 2
    inv_freq = ROPE_THETA ** (-jnp.arange(half, dtype=jnp.float32) / half)
    ang = pos.astype(jnp.float32)[:, None] * inv_freq[None, :]
    cos = jnp.cos(ang)[None, :, None, :]
    sin = jnp.sin(ang)[None, :, None, :]
    x1 = x[..., :half].astype(jnp.float32)
    x2 = x[..., half:].astype(jnp.float32)
    return jnp.concatenate([x1 * cos - x2 * sin, x2 * cos + x1 * sin], -1).astype(x.dtype)


def _dilated_window_attention(q, k, v, window, dilation):
    B, S, H, D = q.shape
    span = window // dilation
    L = S // dilation
    nb = -(-L // span)
    Lp = nb * span

    def to_sub(t):
        t = t.reshape(B, L, dilation, H, D).transpose(0, 2, 1, 3, 4).reshape(B * dilation, L, H, D)
        t = jnp.pad(t, ((0, 0), (0, Lp - L), (0, 0), (0, 0)))
        return t.reshape(B * dilation, nb, span, H, D)

    def band(tb):
        prev = jnp.pad(tb[:, :-1], ((0, 0), (1, 0), (0, 0), (0, 0), (0, 0)))
        return jnp.concatenate([prev, tb], axis=2)

    qb = to_sub(q)
    kb = band(to_sub(k))
    vb = band(to_sub(v))
    s = jnp.einsum('znqhd,znkhd->znhqk', qb, kb).astype(jnp.float32) * (D ** -0.5)
    qi = jnp.arange(span)[:, None]
    kj = jnp.arange(2 * span)[None, :]
    dist = span + qi - kj
    in_band = (dist >= 0) & (dist <= span)
    has_prev = (jnp.arange(nb)[:, None, None] > 0) | (kj >= span)[None]
    mask = (in_band[None] & has_prev)[None, :, None]
    s = jnp.where(mask, s, -jnp.inf)
    m = jnp.max(s, -1, keepdims=True)
    p = jnp.exp(s - m)
    den = jnp.sum(p, -1, keepdims=True)
    o = jnp.einsum('znhqk,znkhd->znqhd', (p / den).astype(v.dtype), vb)
    lse = (m + jnp.log(den))[..., 0].transpose(0, 1, 3, 2)

    def from_sub(t):
        t = t.reshape((B * dilation, Lp) + t.shape[3:])[:, :L]
        t = t.reshape((B, dilation, L) + t.shape[2:])
        t = jnp.moveaxis(t, 1, 2)
        return t.reshape((B, S) + t.shape[3:])

    return from_sub(o), from_sub(lse)


def _dilated_mixture(q, k, v):
    outs, lses = [], []
    for window, dilation in DILATED_PATTERNS:
        o, lse = _dilated_window_attention(q, k, v, window, dilation)
        outs.append(o)
        lses.append(lse)
    w = jax.nn.softmax(jnp.stack(lses), axis=0)
    o = jnp.sum(w[..., None] * jnp.stack(outs).astype(jnp.float32), axis=0)
    return o.astype(q.dtype)


def _mla_attention(c_q, c_kv, k_r, q_a_g, kv_a_g, w_uq, w_ukv, pos):
    B, S, _ = c_q.shape
    q = jnp.einsum('bsr,re->bse', _rms_norm(c_q, q_a_g), w_uq).reshape(B, S, MLA_HEADS, QK_NOPE + QK_ROPE)
    q_nope, q_pe = q[..., :QK_NOPE], q[..., QK_NOPE:]
    q_pe = _rope(q_pe, pos)
    kv = jnp.einsum('bsr,re->bse', _rms_norm(c_kv, kv_a_g), w_ukv).reshape(B, S, MLA_HEADS, QK_NOPE + V_DIM)
    k_nope, v = kv[..., :QK_NOPE], kv[..., QK_NOPE:]
    k_pe = _rope(k_r[:, :, None, :], pos)
    q_full = jnp.concatenate([q_nope, q_pe], -1)
    k_full = jnp.concatenate([k_nope, jnp.broadcast_to(k_pe, (B, S, MLA_HEADS, QK_ROPE))], -1)
    scale = (QK_NOPE + QK_ROPE) ** -0.5
    nq = S // Q_BLOCK
    qb = q_full.reshape(B, nq, Q_BLOCK, MLA_HEADS, QK_NOPE + QK_ROPE).transpose(1, 0, 2, 3, 4)
    kpos = jnp.arange(S)

    def one_block(args):
        i, qblk = args
        s = jnp.einsum('bqhd,bkhd->bhqk', qblk, k_full).astype(jnp.float32) * scale
        qpos = i * Q_BLOCK + jnp.arange(Q_BLOCK)
        s = jnp.where(kpos[None, :] <= qpos[:, None], s, -jnp.inf)
        p = jax.nn.softmax(s, axis=-1)
        return jnp.einsum('bhqk,bkhd->bqhd', p.astype(v.dtype), v)

    o = lax.map(one_block, (jnp.arange(nq), qb))
    return o.transpose(1, 0, 2, 3, 4).reshape(B, S, MLA_WIDTH)


def setup_inputs(seed: int = 0) -> dict:
    key = jax.random.key(seed)
    ks = jax.random.split(key, 16)

    def nrm(k, shape, scale):
        return jax.random.normal(k, shape, jnp.float32) * scale

    return {
        'x': nrm(ks[0], (BATCH, SEQ, D_MODEL), 1.0),
        'w_in': nrm(ks[1], (DEPTH, D_MODEL, IN_COLS), D_MODEL ** -0.5),
        'q_a_norm': 1.0 + nrm(ks[2], (DEPTH, Q_LORA), 0.02),
        'kv_a_norm': 1.0 + nrm(ks[3], (DEPTH, KV_LORA), 0.02),
        'w_uq': nrm(ks[4], (DEPTH, Q_LORA, MLA_HEADS * (QK_NOPE + QK_ROPE)), Q_LORA ** -0.5),
        'w_ukv': nrm(ks[5], (DEPTH, KV_LORA, MLA_HEADS * (QK_NOPE + V_DIM)), KV_LORA ** -0.5),
        'a_out_norm': 1.0 + nrm(ks[6], (DEPTH, A_WIDTH), 0.02),
        'b_out_norm': 1.0 + nrm(ks[7], (DEPTH, MLA_WIDTH), 0.02),
        'w_o': nrm(ks[8], (DEPTH, MIX_WIDTH, D_MODEL), MIX_WIDTH ** -0.5 * BETA_INIT),
        'ln1_g': 1.0 + nrm(ks[9], (DEPTH, D_MODEL), 0.02),
        'ln1_b': nrm(ks[10], (DEPTH, D_MODEL), 0.02),
        'w_ff1': nrm(ks[11], (DEPTH, D_MODEL, D_FF), D_MODEL ** -0.5),
        'w_ff2': nrm(ks[12], (DEPTH, D_FF, D_MODEL), D_FF ** -0.5 * BETA_INIT),
        'ln2_g': 1.0 + nrm(ks[13], (DEPTH, D_MODEL), 0.02),
        'ln2_b': nrm(ks[14], (DEPTH, D_MODEL), 0.02),
    }


def reference(x, w_in, q_a_norm, kv_a_norm, w_uq, w_ukv, a_out_norm, b_out_norm, w_o,
              ln1_g, ln1_b, w_ff1, w_ff2, ln2_g, ln2_b):
    B, S, _ = x.shape
    pos = jnp.arange(S, dtype=jnp.int32)
    splits = [A_WIDTH, 2 * A_WIDTH, 3 * A_WIDTH, 3 * A_WIDTH + Q_LORA, 3 * A_WIDTH + Q_LORA + KV_LORA]
    for l in range(DEPTH):
        h = jnp.einsum('bsd,de->bse', x, w_in[l])
        qa, ka, va, c_q, c_kv, k_r = jnp.split(h, splits, axis=-1)
        qa = _rope(qa.reshape(B, S, A_HEADS, A_HEAD_DIM), pos)
        ka = _rope(ka.reshape(B, S, A_HEADS, A_HEAD_DIM), pos)
        va = va.reshape(B, S, A_HEADS, A_HEAD_DIM)
        a_out = _dilated_mixture(qa, ka, va).reshape(B, S, A_WIDTH)
        b_out = _mla_attention(c_q, c_kv, k_r, q_a_norm[l], kv_a_norm[l], w_uq[l], w_ukv[l], pos)
        mixed = jnp.concatenate([_rms_norm(a_out, a_out_norm[l]), _rms_norm(b_out, b_out_norm[l])], -1)
        y = jnp.einsum('bse,ed->bsd', mixed, w_o[l])
        x = _layer_norm(ALPHA * x + y, ln1_g[l], ln1_b[l])
        f = jnp.square(jax.nn.relu(jnp.einsum('bsd,df->bsf', x, w_ff1[l])))
        f = jnp.einsum('bsf,fd->bsd', f, w_ff2[l])
        x = _layer_norm(ALPHA * x + f, ln2_g[l], ln2_b[l])
    return x
```

```python
import functools

import jax
import jax.numpy as jnp
from jax import lax
from jax.experimental import pallas as pl
from jax.experimental.pallas import tpu as pltpu

D_MODEL = 1024
DEPTH = 4
A_HEADS = 8
A_HEAD_DIM = 64
A_WIDTH = A_HEADS * A_HEAD_DIM
DILATED_PATTERNS = ((128, 1), (512, 4), (2048, 16))
MLA_HEADS = 4
QK_NOPE = 128
QK_ROPE = 64
V_DIM = 128
Q_LORA = 256
KV_LORA = 128
MLA_WIDTH = MLA_HEADS * V_DIM
D_FF = 4 * D_MODEL
ROPE_THETA = 10000.0
ALPHA = (2.0 * DEPTH) ** 0.25
LN_EPS = 1e-5
RMS_EPS = 1e-6

LANES = 128
MLA_KPAD = 256
IN_COLS_PAD = 3 * A_WIDTH + Q_LORA + KV_LORA + LANES
VMEM_LIMIT = 56 * 1024 * 1024

F32 = jnp.float32
BF16 = jnp.bfloat16
NEG_INF = float("-inf")


def _const_spec(shape):
    return pl.BlockSpec(shape, lambda *_: (0,) * len(shape), pipeline_mode=pl.Buffered(1))


def _params(n_axes):
    return pltpu.CompilerParams(dimension_semantics=("arbitrary",) * n_axes,
                                vmem_limit_bytes=VMEM_LIMIT)


def _rms(x, g):
    return x * lax.rsqrt(jnp.mean(x * x, -1, keepdims=True) + RMS_EPS) * g


def _layer_norm(z, g, b):
    mu = jnp.mean(z, -1, keepdims=True)
    zc = z - mu
    var = jnp.mean(zc * zc, -1, keepdims=True)
    return zc * lax.rsqrt(var + LN_EPS) * g + b


def _rope128(x, cos, sin_signed, first_half):
    rot = jnp.where(first_half, pltpu.roll(x, 96, 1), pltpu.roll(x, 32, 1))
    return x * cos + rot * sin_signed


def _in_proj_kernel(x_ref, w_in_ref, w_uq_ref, w_ukv_ref, gq_ref, gkv_ref, cos_ref, sin_ref,
                    qa_ref, ka_ref, va_ref, qm_ref, km_ref, vm_ref):
    h = jnp.dot(x_ref[...].astype(BF16), w_in_ref[...], preferred_element_type=F32)
    cos = cos_ref[...]
    sin = sin_ref[...]
    lane = lax.broadcasted_iota(jnp.int32, cos.shape, 1)
    first_half = (lane & (A_HEAD_DIM - 1)) < (A_HEAD_DIM // 2)

    def rope(t):
        return _rope128(t, cos, sin, first_half)

    def rope_cols(t):
        return jnp.concatenate(
            [rope(t[:, c * LANES:(c + 1) * LANES]) for c in range(t.shape[1] // LANES)], axis=1)

    qa_ref[...] = (rope_cols(h[:, 0:A_WIDTH]) * (A_HEAD_DIM ** -0.5)).astype(BF16)
    ka_ref[...] = rope_cols(h[:, A_WIDTH:2 * A_WIDTH]).astype(BF16)
    va_ref[...] = h[:, 2 * A_WIDTH:3 * A_WIDTH].astype(BF16)

    c0 = 3 * A_WIDTH
    cq = _rms(h[:, c0:c0 + Q_LORA], gq_ref[...])
    q = jnp.dot(cq.astype(BF16), w_uq_ref[...], preferred_element_type=F32)
    q_parts = []
    for hh in range(MLA_HEADS):
        base = hh * MLA_KPAD
        q_parts.append(q[:, base:base + QK_NOPE])
        q_parts.append(rope(q[:, base + QK_NOPE:base + MLA_KPAD]))
    qm_ref[...] = jnp.concatenate(q_parts, axis=1).astype(BF16)

    c1 = c0 + Q_LORA
    ckv = _rms(h[:, c1:c1 + KV_LORA], gkv_ref[...])
    kv = jnp.dot(ckv.astype(BF16), w_ukv_ref[...], preferred_element_type=F32)
    c2 = c1 + KV_LORA
    k_pe = rope(h[:, c2:c2 + LANES])
    k_parts = []
    for hh in range(MLA_HEADS):
        k_parts.append(kv[:, hh * QK_NOPE:(hh + 1) * QK_NOPE])
        k_parts.append(k_pe)
    km_ref[...] = jnp.concatenate(k_parts, axis=1).astype(BF16)
    vm_ref[...] = kv[:, MLA_HEADS * QK_NOPE:].astype(BF16)


def _in_proj(x, w_in, w_uq, w_ukv, gq, gkv, cos, sin, *, seq, tm):
    t = x.shape[0]
    n_pos_blocks = seq // tm
    row = lambda i: (i, 0)
    pos = lambda i: (i % n_pos_blocks, 0)
    out_widths = (A_WIDTH, A_WIDTH, A_WIDTH, MLA_HEADS * MLA_KPAD, MLA_HEADS * MLA_KPAD, MLA_WIDTH)
    return pl.pallas_call(
        _in_proj_kernel,
        grid=(t // tm,),
        in_specs=[
            pl.BlockSpec((tm, D_MODEL), row),
            _const_spec(w_in.shape), _const_spec(w_uq.shape), _const_spec(w_ukv.shape),
            _const_spec(gq.shape), _const_spec(gkv.shape),
            pl.BlockSpec((tm, LANES), pos), pl.BlockSpec((tm, LANES), pos),
        ],
        out_specs=[pl.BlockSpec((tm, w), row) for w in out_widths],
        out_shape=[jax.ShapeDtypeStruct((t, w), BF16) for w in out_widths],
        compiler_params=_params(1),
        name="in_proj",
    )(x, w_in, w_uq, w_ukv, gq, gkv, cos, sin)


def _mla_kernel(q_ref, k_ref, v_ref, o_ref, m_sc, l_sc, acc_sc, *, scale):
    qi = pl.program_id(2)
    ki = pl.program_id(3)

    @pl.when(ki == 0)
    def _():
        m_sc[...] = jnp.full(m_sc.shape, NEG_INF, F32)
        l_sc[...] = jnp.zeros(l_sc.shape, F32)
        acc_sc[...] = jnp.zeros(acc_sc.shape, F32)

    def step(masked):
        s = lax.dot_general(q_ref[...], k_ref[...], (((1,), (1,)), ((), ())),
                            preferred_element_type=F32) * scale
        if masked:
            r = lax.broadcasted_iota(jnp.int32, s.shape, 0)
            c = lax.broadcasted_iota(jnp.int32, s.shape, 1)
            s = jnp.where(c <= r, s, NEG_INF)
        m_prev = m_sc[...]
        m_new = jnp.maximum(m_prev, jnp.max(s, -1, keepdims=True))
        alpha = jnp.exp(m_prev - m_new)
        p = jnp.exp(s - m_new)
        l_sc[...] = alpha * l_sc[...] + jnp.sum(p, -1, keepdims=True)
        acc_sc[...] = alpha * acc_sc[...] + jnp.dot(p.astype(BF16), v_ref[...],
                                                    preferred_element_type=F32)
        m_sc[...] = m_new

    @pl.when(ki < qi)
    def _():
        step(False)

    @pl.when(ki == qi)
    def _():
        step(True)
        o_ref[...] = acc_sc[...] / l_sc[...]


def _mla_attention(qm, km, vm, *, batch, seq, tq):
    t = qm.shape[0]
    nq = seq // tq
    q_map = lambda b, h, qi, ki: (b * nq + qi, h)
    kv_map = lambda b, h, qi, ki: (b * nq + jnp.minimum(ki, qi), h)
    return pl.pallas_call(
        functools.partial(_mla_kernel, scale=(QK_NOPE + QK_ROPE) ** -0.5),
        grid=(batch, MLA_HEADS, nq, nq),
        in_specs=[
            pl.BlockSpec((tq, MLA_KPAD), q_map),
            pl.BlockSpec((tq, MLA_KPAD), kv_map),
            pl.BlockSpec((tq, V_DIM), kv_map),
        ],
        out_specs=pl.BlockSpec((tq, V_DIM), q_map),
        out_shape=jax.ShapeDtypeStruct((t, MLA_WIDTH), F32),
        scratch_shapes=[pltpu.VMEM((tq, 1), F32), pltpu.VMEM((tq, 1), F32),
                        pltpu.VMEM((tq, V_DIM), F32)],
        compiler_params=_params(4),
        name="mla_attention",
    )(qm, km, vm)


SPAN = 128
LSE_REP = LANES // A_HEADS


def _dilated_kernel(*refs, tq, chained):
    if chained:
        q_ref, kp_ref, ko_ref, vp_ref, vo_ref, op_ref, lp_ref, o_ref, l_ref = refs
    else:
        q_ref, kp_ref, ko_ref, vp_ref, vo_ref, o_ref, l_ref = refs
    i = pl.program_id(2)
    qi = lax.broadcasted_iota(jnp.int32, (SPAN, 2 * SPAN), 0)
    kj = lax.broadcasted_iota(jnp.int32, (SPAN, 2 * SPAN), 1)
    band = (kj >= qi) & (kj <= qi + SPAN)
    band_first = band & ((kj >= SPAN) | (i > 0))
    head_of_lane = lax.broadcasted_iota(jnp.int32, (SPAN, LANES), 1) // LSE_REP

    for j in range(tq // SPAN):
        rows = slice(j * SPAN, (j + 1) * SPAN)
        if j == 0:
            k_blk = jnp.concatenate([kp_ref[...], ko_ref[0:SPAN, :]], axis=0)
            v_blk = jnp.concatenate([vp_ref[...], vo_ref[0:SPAN, :]], axis=0)
            mask = band_first
        else:
            k_blk = ko_ref[(j - 1) * SPAN:(j + 1) * SPAN, :]
            v_blk = vo_ref[(j - 1) * SPAN:(j + 1) * SPAN, :]
            mask = band
        q_blk = q_ref[rows, :]
        lse_tile = jnp.zeros((SPAN, LANES), F32)
        for h in range(A_HEADS):
            cols = slice(h * A_HEAD_DIM, (h + 1) * A_HEAD_DIM)
            s = lax.dot_general(q_blk[:, cols], k_blk[:, cols], (((1,), (1,)), ((), ())),
                                preferred_element_type=F32)
            s = jnp.where(mask, s, NEG_INF)
            m = jnp.max(s, -1, keepdims=True)
            p = jnp.exp(s - m)
            den = jnp.sum(p, -1, keepdims=True)
            o = jnp.dot(p.astype(BF16), v_blk[:, cols], preferred_element_type=F32) / den
            lse = m + jnp.log(den)
            if chained:
                lse_prev = lp_ref[rows, h * LSE_REP:h * LSE_REP + 1]
                mx = jnp.maximum(lse_prev, lse)
                w_prev = jnp.exp(lse_prev - mx)
                w_cur = jnp.exp(lse - mx)
                tot = w_prev + w_cur
                o = (w_prev * op_ref[rows, cols] + w_cur * o) / tot
                lse = mx + jnp.log(tot)
            o_ref[rows, cols] = o
            lse_tile = jnp.where(head_of_lane == h, lse, lse_tile)
        l_ref[rows, :] = lse_tile


def _dilated_pattern(qa, ka, va, state, *, batch, seq, dilation):
    sub_len = seq // dilation
    tq = min(sub_len, 512)
    view = lambda a, w: a.reshape(batch, sub_len, dilation * w)
    own = lambda b, r, i: (b, i, r)
    prev = lambda b, r, i: (b, jnp.maximum(i * (tq // SPAN) - 1, 0), r)
    wide = pl.BlockSpec((None, tq, A_WIDTH), own)
    wide_prev = pl.BlockSpec((None, SPAN, A_WIDTH), prev)
    narrow = pl.BlockSpec((None, tq, LANES), own)
    in_specs = [wide, wide_prev, wide, wide_prev, wide]
    args = [view(qa, A_WIDTH), view(ka, A_WIDTH), view(ka, A_WIDTH), view(va, A_WIDTH), view(va, A_WIDTH)]
    if state is not None:
        in_specs += [wide, narrow]
        args += [view(state[0], A_WIDTH), view(state[1], LANES)]
    o, lse = pl.pallas_call(
        functools.partial(_dilated_kernel, tq=tq, chained=state is not None),
        grid=(batch, dilation, sub_len // tq),
        in_specs=in_specs,
        out_specs=[wide, narrow],
        out_shape=[jax.ShapeDtypeStruct((batch, sub_len, dilation * A_WIDTH), F32),
                   jax.ShapeDtypeStruct((batch, sub_len, dilation * LANES), F32)],
        compiler_params=_params(3),
        name=f"dilated_d{dilation}",
    )(*args)
    t = batch * seq
    return o.reshape(t, A_WIDTH), lse.reshape(t, LANES)


def _out_proj_kernel(a_ref, b_ref, x_ref, wo_ref, ga_ref, gb_ref, g_ref, beta_ref, o_ref):
    mixed = jnp.concatenate([_rms(a_ref[...], ga_ref[...]), _rms(b_ref[...], gb_ref[...])], axis=1)
    y = jnp.dot(mixed.astype(BF16), wo_ref[...], preferred_element_type=F32)
    o_ref[...] = _layer_norm(ALPHA * x_ref[...] + y, g_ref[...], beta_ref[...])


def _out_proj(a, b, x, wo, ga, gb, g, beta, *, tm):
    t = x.shape[0]
    row = lambda i: (i, 0)
    return pl.pallas_call(
        _out_proj_kernel,
        grid=(t // tm,),
        in_specs=[pl.BlockSpec((tm, A_WIDTH), row), pl.BlockSpec((tm, MLA_WIDTH), row),
                  pl.BlockSpec((tm, D_MODEL), row), _const_spec(wo.shape),
                  _const_spec(ga.shape), _const_spec(gb.shape), _const_spec(g.shape),
                  _const_spec(beta.shape)],
        out_specs=pl.BlockSpec((tm, D_MODEL), row),
        out_shape=jax.ShapeDtypeStruct((t, D_MODEL), F32),
        compiler_params=_params(1),
        name="out_proj",
    )(a, b, x, wo, ga, gb, g, beta)


FF_CHUNK = 1024


def _ffn_kernel(x_ref, w1_ref, w2_ref, g_ref, beta_ref, o_ref):
    x = x_ref[...]
    xb = x.astype(BF16)
    acc = ALPHA * x
    for c in range(D_FF // FF_CHUNK):
        cols = slice(c * FF_CHUNK, (c + 1) * FF_CHUNK)
        hdn = jnp.dot(xb, w1_ref[:, cols], preferred_element_type=F32)
        hdn = jnp.square(jnp.maximum(hdn, 0.0)).astype(BF16)
        acc = acc + jnp.dot(hdn, w2_ref[cols, :], preferred_element_type=F32)
    o_ref[...] = _layer_norm(acc, g_ref[...], beta_ref[...])


def _ffn(x, w1, w2, g, beta, *, tm):
    t = x.shape[0]
    row = lambda i: (i, 0)
    return pl.pallas_call(
        _ffn_kernel,
        grid=(t // tm,),
        in_specs=[pl.BlockSpec((tm, D_MODEL), row), _const_spec(w1.shape), _const_spec(w2.shape),
                  _const_spec(g.shape), _const_spec(beta.shape)],
        out_specs=pl.BlockSpec((tm, D_MODEL), row),
        out_shape=jax.ShapeDtypeStruct((t, D_MODEL), F32),
        compiler_params=_params(1),
        name="ffn",
    )(x, w1, w2, g, beta)


def _rope_tables(seq):
    half = A_HEAD_DIM // 2
    inv_freq = ROPE_THETA ** (-jnp.arange(half, dtype=F32) / half)
    ang = jnp.arange(seq, dtype=jnp.int32).astype(F32)[:, None] * inv_freq[None, :]
    cos = jnp.cos(ang)
    sin = jnp.sin(ang)
    cos128 = jnp.tile(cos, (1, LANES // half))
    sin128 = jnp.tile(jnp.concatenate([-sin, sin], axis=1), (1, LANES // A_HEAD_DIM))
    return cos128, sin128


def _prep_w_uq(w):
    w = w.reshape(Q_LORA, MLA_HEADS, QK_NOPE + QK_ROPE)
    w = jnp.pad(w, ((0, 0), (0, 0), (0, MLA_KPAD - QK_NOPE - QK_ROPE)))
    return w.reshape(Q_LORA, MLA_HEADS * MLA_KPAD).astype(BF16)


def _prep_w_ukv(w):
    w = w.reshape(KV_LORA, MLA_HEADS, QK_NOPE + V_DIM)
    k = w[:, :, :QK_NOPE].reshape(KV_LORA, MLA_HEADS * QK_NOPE)
    v = w[:, :, QK_NOPE:].reshape(KV_LORA, MLA_HEADS * V_DIM)
    return jnp.concatenate([k, v], axis=1).astype(BF16)


def kernel(x, w_in, q_a_norm, kv_a_norm, w_uq, w_ukv, a_out_norm, b_out_norm, w_o,
           ln1_g, ln1_b, w_ff1, w_ff2, ln2_g, ln2_b):
    batch, seq, _ = x.shape
    t = batch * seq
    cos, sin = _rope_tables(seq)
    xf = x.reshape(t, D_MODEL)
    row2d = lambda v: v.reshape(1, -1)
    for l in range(DEPTH):
        w_in_l = jnp.pad(w_in[l], ((0, 0), (0, IN_COLS_PAD - w_in.shape[2]))).astype(BF16)
        qa, ka, va, qm, km, vm = _in_proj(
            xf, w_in_l, _prep_w_uq(w_uq[l]), _prep_w_ukv(w_ukv[l]),
            row2d(q_a_norm[l]), row2d(kv_a_norm[l]), cos, sin, seq=seq, tm=512)
        b_out = _mla_attention(qm, km, vm, batch=batch, seq=seq, tq=512)
        state = None
        for _, dilation in DILATED_PATTERNS:
            state = _dilated_pattern(qa, ka, va, state, batch=batch, seq=seq, dilation=dilation)
        a_out = state[0]
        xf = _out_proj(a_out, b_out, xf, w_o[l].astype(BF16), row2d(a_out_norm[l]),
                       row2d(b_out_norm[l]), row2d(ln1_g[l]), row2d(ln1_b[l]), tm=512)
        xf = _ffn(xf, w_ff1[l].astype(BF16), w_ff2[l].astype(BF16), row2d(ln2_g[l]),
                  row2d(ln2_b[l]), tm=512)
    return xf.reshape(batch, seq, D_MODEL)
```

```python
import functools

import jax
import jax.numpy as jnp
from jax import lax
from jax.experimental import pallas as pl
from jax.experimental.pallas import tpu as pltpu

D_MODEL = 1024
DEPTH = 4
A_HEADS = 8
A_HEAD_DIM = 64
A_WIDTH = A_HEADS * A_HEAD_DIM
DILATED_PATTERNS = ((128, 1), (512, 4), (2048, 16))
MLA_HEADS = 4
QK_NOPE = 128
QK_ROPE = 64
V_DIM = 128
Q_LORA = 256
KV_LORA = 128
MLA_WIDTH = MLA_HEADS * V_DIM
D_FF = 4 * D_MODEL
ROPE_THETA = 10000.0
ALPHA = (2.0 * DEPTH) ** 0.25
LN_EPS = 1e-5
RMS_EPS = 1e-6

LANES = 128
MLA_KPAD = 256
IN_COLS_PAD = 3 * A_WIDTH + Q_LORA + KV_LORA + LANES
VMEM_LIMIT = 56 * 1024 * 1024

F32 = jnp.float32
BF16 = jnp.bfloat16
NEG_INF = float("-inf")
LOG2E = 1.4426950408889634
MLA_Q_SCALE = (QK_NOPE + QK_ROPE) ** -0.5 * LOG2E


def _const_spec(shape):
    return pl.BlockSpec(shape, lambda *_: (0,) * len(shape), pipeline_mode=pl.Buffered(1))


def _params(n_axes):
    return pltpu.CompilerParams(dimension_semantics=("arbitrary",) * n_axes,
                                vmem_limit_bytes=VMEM_LIMIT)


def _rms(x, g):
    return x * lax.rsqrt(jnp.mean(x * x, -1, keepdims=True) + RMS_EPS) * g


def _layer_norm(z, g, b):
    mu = jnp.mean(z, -1, keepdims=True)
    zc = z - mu
    var = jnp.mean(zc * zc, -1, keepdims=True)
    return zc * lax.rsqrt(var + LN_EPS) * g + b


def _rope128(x, cos, sin_signed, first_half):
    rot = jnp.where(first_half, pltpu.roll(x, 96, 1), pltpu.roll(x, 32, 1))
    return x * cos + rot * sin_signed


def _in_proj_kernel(x_ref, w_in_ref, w_uq_ref, w_ukv_ref, gq_ref, gkv_ref, cos_ref, sin_ref,
                    qa_ref, ka_ref, va_ref, qm_ref, km_ref, vm_ref):
    h = jnp.dot(x_ref[...].astype(BF16), w_in_ref[...], preferred_element_type=F32)
    cos = cos_ref[...]
    sin = sin_ref[...]
    lane = lax.broadcasted_iota(jnp.int32, cos.shape, 1)
    first_half = (lane & (A_HEAD_DIM - 1)) < (A_HEAD_DIM // 2)

    def rope(t):
        return _rope128(t, cos, sin, first_half)

    def rope_cols(t):
        return jnp.concatenate(
            [rope(t[:, c * LANES:(c + 1) * LANES]) for c in range(t.shape[1] // LANES)], axis=1)

    qa_ref[...] = (rope_cols(h[:, 0:A_WIDTH]) * (A_HEAD_DIM ** -0.5)).astype(BF16)
    ka_ref[...] = rope_cols(h[:, A_WIDTH:2 * A_WIDTH]).astype(BF16)
    va_ref[...] = h[:, 2 * A_WIDTH:3 * A_WIDTH].astype(BF16)

    c0 = 3 * A_WIDTH
    cq = _rms(h[:, c0:c0 + Q_LORA], gq_ref[...])
    q = jnp.dot(cq.astype(BF16), w_uq_ref[...], preferred_element_type=F32)
    q_parts = []
    for hh in range(MLA_HEADS):
        base = hh * MLA_KPAD
        q_parts.append(q[:, base:base + QK_NOPE])
        q_parts.append(rope(q[:, base + QK_NOPE:base + MLA_KPAD]))
    qm_ref[...] = (jnp.concatenate(q_parts, axis=1) * MLA_Q_SCALE).astype(BF16)

    c1 = c0 + Q_LORA
    ckv = _rms(h[:, c1:c1 + KV_LORA], gkv_ref[...])
    kv = jnp.dot(ckv.astype(BF16), w_ukv_ref[...], preferred_element_type=F32)
    c2 = c1 + KV_LORA
    k_pe = rope(h[:, c2:c2 + LANES])
    k_parts = []
    for hh in range(MLA_HEADS):
        k_parts.append(kv[:, hh * QK_NOPE:(hh + 1) * QK_NOPE])
        k_parts.append(k_pe)
    km_ref[...] = jnp.concatenate(k_parts, axis=1).astype(BF16)
    vm_ref[...] = kv[:, MLA_HEADS * QK_NOPE:].astype(BF16)


def _in_proj(x, w_in, w_uq, w_ukv, gq, gkv, cos, sin, *, seq, tm):
    t = x.shape[0]
    n_pos_blocks = seq // tm
    row = lambda i: (i, 0)
    pos = lambda i: (i % n_pos_blocks, 0)
    out_widths = (A_WIDTH, A_WIDTH, A_WIDTH, MLA_HEADS * MLA_KPAD, MLA_HEADS * MLA_KPAD, MLA_WIDTH)
    return pl.pallas_call(
        _in_proj_kernel,
        grid=(t // tm,),
        in_specs=[
            pl.BlockSpec((tm, D_MODEL), row),
            _const_spec(w_in.shape), _const_spec(w_uq.shape), _const_spec(w_ukv.shape),
            _const_spec(gq.shape), _const_spec(gkv.shape),
            pl.BlockSpec((tm, LANES), pos), pl.BlockSpec((tm, LANES), pos),
        ],
        out_specs=[pl.BlockSpec((tm, w), row) for w in out_widths],
        out_shape=[jax.ShapeDtypeStruct((t, w), BF16) for w in out_widths],
        compiler_params=_params(1),
        name="in_proj",
    )(x, w_in, w_uq, w_ukv, gq, gkv, cos, sin)


def _mla_kernel(q_ref, k_ref, v_ref, o_ref, m_sc, acc_sc, *, tk):
    qi = pl.program_id(1)
    m_sc[...] = jnp.full(m_sc.shape, NEG_INF, F32)
    acc_sc[...] = jnp.zeros(acc_sc.shape, F32)

    def tile(k_start, masked):
        for h in range(MLA_HEADS):
            q = q_ref[:, h * MLA_KPAD:(h + 1) * MLA_KPAD]
            k = k_ref[pl.ds(k_start, tk), h * MLA_KPAD:(h + 1) * MLA_KPAD]
            v = v_ref[pl.ds(k_start, tk), h * V_DIM:(h + 1) * V_DIM]
            s = lax.dot_general(q, k, (((1,), (1,)), ((), ())), preferred_element_type=F32)
            if masked:
                r = lax.broadcasted_iota(jnp.int32, s.shape, 0)
                c = lax.broadcasted_iota(jnp.int32, s.shape, 1)
                s = jnp.where(c <= r, s, NEG_INF)
            m_prev = m_sc[h]
            m_new = jnp.maximum(m_prev, jnp.max(s, -1, keepdims=True))
            alpha = jnp.exp2(m_prev - m_new)
            p = jnp.exp2(s - jnp.concatenate([m_new] * (tk // LANES), axis=1))
            v1 = jnp.concatenate([v, jnp.ones_like(v)], axis=1)
            acc_sc[h] = (jnp.concatenate([alpha, alpha], axis=1) * acc_sc[h]
                         + jnp.dot(p.astype(BF16), v1, preferred_element_type=F32))
            m_sc[h] = m_new

    def body(ki, carry):
        tile(pl.multiple_of(ki * tk, tk), False)
        return carry

    lax.fori_loop(0, qi, body, 0)
    tile(pl.multiple_of(qi * tk, tk), True)
    for h in range(MLA_HEADS):
        acc = acc_sc[h]
        o_ref[:, h * V_DIM:(h + 1) * V_DIM] = acc[:, :V_DIM] / acc[:, V_DIM:]


def _mla_attention(qm, km, vm, *, batch, seq, tq):
    t = qm.shape[0]
    nq = seq // tq
    q_map = lambda b, qi: (b * nq + qi, 0)
    kv_map = lambda b, qi: (b, 0)
    return pl.pallas_call(
        functools.partial(_mla_kernel, tk=tq),
        grid=(batch, nq),
        in_specs=[
            pl.BlockSpec((tq, MLA_HEADS * MLA_KPAD), q_map),
            pl.BlockSpec((seq, MLA_HEADS * MLA_KPAD), kv_map),
            pl.BlockSpec((seq, MLA_WIDTH), kv_map),
        ],
        out_specs=pl.BlockSpec((tq, MLA_WIDTH), q_map),
        out_shape=jax.ShapeDtypeStruct((t, MLA_WIDTH), F32),
        scratch_shapes=[pltpu.VMEM((MLA_HEADS, tq, LANES), F32),
                        pltpu.VMEM((MLA_HEADS, tq, 2 * V_DIM), F32)],
        compiler_params=_params(2),
        name="mla_attention",
    )(qm, km, vm)


SPAN = 128
LSE_REP = LANES // A_HEADS


def _dilated_kernel(*refs, tq, chained):
    if chained:
        q_ref, kp_ref, ko_ref, vp_ref, vo_ref, op_ref, lp_ref, o_ref, l_ref = refs
    else:
        q_ref, kp_ref, ko_ref, vp_ref, vo_ref, o_ref, l_ref = refs
    i = pl.program_id(2)
    qi = lax.broadcasted_iota(jnp.int32, (SPAN, 2 * SPAN), 0)
    kj = lax.broadcasted_iota(jnp.int32, (SPAN, 2 * SPAN), 1)
    band = (kj >= qi) & (kj <= qi + SPAN)
    band_first = band & ((kj >= SPAN) | (i > 0))
    head_of_lane = lax.broadcasted_iota(jnp.int32, (SPAN, LANES), 1) // LSE_REP

    for j in range(tq // SPAN):
        rows = slice(j * SPAN, (j + 1) * SPAN)
        if j == 0:
            k_blk = jnp.concatenate([kp_ref[...], ko_ref[0:SPAN, :]], axis=0)
            v_blk = jnp.concatenate([vp_ref[...], vo_ref[0:SPAN, :]], axis=0)
            mask = band_first
        else:
            k_blk = ko_ref[(j - 1) * SPAN:(j + 1) * SPAN, :]
            v_blk = vo_ref[(j - 1) * SPAN:(j + 1) * SPAN, :]
            mask = band
        q_blk = q_ref[rows, :]
        lse_tile = jnp.zeros((SPAN, LANES), F32)
        for h in range(A_HEADS):
            cols = slice(h * A_HEAD_DIM, (h + 1) * A_HEAD_DIM)
            s = lax.dot_general(q_blk[:, cols], k_blk[:, cols], (((1,), (1,)), ((), ())),
                                preferred_element_type=F32)
            s = jnp.where(mask, s, NEG_INF)
            m = jnp.max(s, -1, keepdims=True)
            p = jnp.exp(s - m)
            den = jnp.sum(p, -1, keepdims=True)
            o = jnp.dot(p.astype(BF16), v_blk[:, cols], preferred_element_type=F32) / den
            lse = m + jnp.log(den)
            if chained:
                lse_prev = lp_ref[rows, h * LSE_REP:h * LSE_REP + 1]
                mx = jnp.maximum(lse_prev, lse)
                w_prev = jnp.exp(lse_prev - mx)
                w_cur = jnp.exp(lse - mx)
                tot = w_prev + w_cur
                o = (w_prev * op_ref[rows, cols] + w_cur * o) / tot
                lse = mx + jnp.log(tot)
            o_ref[rows, cols] = o
            lse_tile = jnp.where(head_of_lane == h, lse, lse_tile)
        l_ref[rows, :] = lse_tile


def _dilated_pattern(qa, ka, va, state, *, batch, seq, dilation):
    sub_len = seq // dilation
    tq = min(sub_len, 512)
    view = lambda a, w: a.reshape(batch, sub_len, dilation * w)
    own = lambda b, r, i: (b, i, r)
    prev = lambda b, r, i: (b, jnp.maximum(i * (tq // SPAN) - 1, 0), r)
    wide = pl.BlockSpec((None, tq, A_WIDTH), own)
    wide_prev = pl.BlockSpec((None, SPAN, A_WIDTH), prev)
    narrow = pl.BlockSpec((None, tq, LANES), own)
    in_specs = [wide, wide_prev, wide, wide_prev, wide]
    args = [view(qa, A_WIDTH), view(ka, A_WIDTH), view(ka, A_WIDTH), view(va, A_WIDTH), view(va, A_WIDTH)]
    if state is not None:
        in_specs += [wide, narrow]
        args += [view(state[0], A_WIDTH), view(state[1], LANES)]
    o, lse = pl.pallas_call(
        functools.partial(_dilated_kernel, tq=tq, chained=state is not None),
        grid=(batch, dilation, sub_len // tq),
        in_specs=in_specs,
        out_specs=[wide, narrow],
        out_shape=[jax.ShapeDtypeStruct((batch, sub_len, dilation * A_WIDTH), F32),
                   jax.ShapeDtypeStruct((batch, sub_len, dilation * LANES), F32)],
        compiler_params=_params(3),
        name=f"dilated_d{dilation}",
    )(*args)
    t = batch * seq
    return o.reshape(t, A_WIDTH), lse.reshape(t, LANES)


def _out_proj_kernel(a_ref, b_ref, x_ref, wo_ref, ga_ref, gb_ref, g_ref, beta_ref, o_ref):
    mixed = jnp.concatenate([_rms(a_ref[...], ga_ref[...]), _rms(b_ref[...], gb_ref[...])], axis=1)
    y = jnp.dot(mixed.astype(BF16), wo_ref[...], preferred_element_type=F32)
    o_ref[...] = _layer_norm(ALPHA * x_ref[...] + y, g_ref[...], beta_ref[...])


def _out_proj(a, b, x, wo, ga, gb, g, beta, *, tm):
    t = x.shape[0]
    row = lambda i: (i, 0)
    return pl.pallas_call(
        _out_proj_kernel,
        grid=(t // tm,),
        in_specs=[pl.BlockSpec((tm, A_WIDTH), row), pl.BlockSpec((tm, MLA_WIDTH), row),
                  pl.BlockSpec((tm, D_MODEL), row), _const_spec(wo.shape),
                  _const_spec(ga.shape), _const_spec(gb.shape), _const_spec(g.shape),
                  _const_spec(beta.shape)],
        out_specs=pl.BlockSpec((tm, D_MODEL), row),
        out_shape=jax.ShapeDtypeStruct((t, D_MODEL), F32),
        compiler_params=_params(1),
        name="out_proj",
    )(a, b, x, wo, ga, gb, g, beta)


FF_CHUNK = 1024


def _ffn_kernel(x_ref, w1_ref, w2_ref, g_ref, beta_ref, o_ref):
    x = x_ref[...]
    xb = x.astype(BF16)
    acc = ALPHA * x
    for c in range(D_FF // FF_CHUNK):
        cols = slice(c * FF_CHUNK, (c + 1) * FF_CHUNK)
        hdn = jnp.dot(xb, w1_ref[:, cols], preferred_element_type=F32)
        hdn = jnp.square(jnp.maximum(hdn, 0.0)).astype(BF16)
        acc = acc + jnp.dot(hdn, w2_ref[cols, :], preferred_element_type=F32)
    o_ref[...] = _layer_norm(acc, g_ref[...], beta_ref[...])


def _ffn(x, w1, w2, g, beta, *, tm):
    t = x.shape[0]
    row = lambda i: (i, 0)
    return pl.pallas_call(
        _ffn_kernel,
        grid=(t // tm,),
        in_specs=[pl.BlockSpec((tm, D_MODEL), row), _const_spec(w1.shape), _const_spec(w2.shape),
                  _const_spec(g.shape), _const_spec(beta.shape)],
        out_specs=pl.BlockSpec((tm, D_MODEL), row),
        out_shape=jax.ShapeDtypeStruct((t, D_MODEL), F32),
        compiler_params=_params(1),
        name="ffn",
    )(x, w1, w2, g, beta)


def _rope_tables(seq):
    half = A_HEAD_DIM // 2
    inv_freq = ROPE_THETA ** (-jnp.arange(half, dtype=F32) / half)
    ang = jnp.arange(seq, dtype=jnp.int32).astype(F32)[:, None] * inv_freq[None, :]
    cos = jnp.cos(ang)
    sin = jnp.sin(ang)
    cos128 = jnp.tile(cos, (1, LANES // half))
    sin128 = jnp.tile(jnp.concatenate([-sin, sin], axis=1), (1, LANES // A_HEAD_DIM))
    return cos128, sin128


def _prep_w_uq(w):
    w = w.reshape(Q_LORA, MLA_HEADS, QK_NOPE + QK_ROPE)
    w = jnp.pad(w, ((0, 0), (0, 0), (0, MLA_KPAD - QK_NOPE - QK_ROPE)))
    return w.reshape(Q_LORA, MLA_HEADS * MLA_KPAD).astype(BF16)


def _prep_w_ukv(w):
    w = w.reshape(KV_LORA, MLA_HEADS, QK_NOPE + V_DIM)
    k = w[:, :, :QK_NOPE].reshape(KV_LORA, MLA_HEADS * QK_NOPE)
    v = w[:, :, QK_NOPE:].reshape(KV_LORA, MLA_HEADS * V_DIM)
    return jnp.concatenate([k, v], axis=1).astype(BF16)


def kernel(x, w_in, q_a_norm, kv_a_norm, w_uq, w_ukv, a_out_norm, b_out_norm, w_o,
           ln1_g, ln1_b, w_ff1, w_ff2, ln2_g, ln2_b):
    batch, seq, _ = x.shape
    t = batch * seq
    cos, sin = _rope_tables(seq)
    xf = x.reshape(t, D_MODEL)
    row2d = lambda v: v.reshape(1, -1)
    for l in range(DEPTH):
        w_in_l = jnp.pad(w_in[l], ((0, 0), (0, IN_COLS_PAD - w_in.shape[2]))).astype(BF16)
        qa, ka, va, qm, km, vm = _in_proj(
            xf, w_in_l, _prep_w_uq(w_uq[l]), _prep_w_ukv(w_ukv[l]),
            row2d(q_a_norm[l]), row2d(kv_a_norm[l]), cos, sin, seq=seq, tm=512)
        b_out = _mla_attention(qm, km, vm, batch=batch, seq=seq, tq=512)
        state = None
        for _, dilation in DILATED_PATTERNS:
            state = _dilated_pattern(qa, ka, va, state, batch=batch, seq=seq, dilation=dilation)
        a_out = state[0]
        xf = _out_proj(a_out, b_out, xf, w_o[l].astype(BF16), row2d(a_out_norm[l]),
                       row2d(b_out_norm[l]), row2d(ln1_g[l]), row2d(ln1_b[l]), tm=512)
        xf = _ffn(xf, w_ff1[l].astype(BF16), w_ff2[l].astype(BF16), row2d(ln2_g[l]),
                  row2d(ln2_b[l]), tm=512)
    return xf.reshape(batch, seq, D_MODEL)
```

```python
import functools

import jax
import jax.numpy as jnp
from jax import lax
from jax.experimental import pallas as pl
from jax.experimental.pallas import tpu as pltpu

D_MODEL = 1024
DEPTH = 4
A_HEADS = 8
A_HEAD_DIM = 64
A_WIDTH = A_HEADS * A_HEAD_DIM
MLA_HEADS = 4
QK_NOPE = 128
QK_ROPE = 64
V_DIM = 128
Q_LORA = 256
KV_LORA = 128
MLA_WIDTH = MLA_HEADS * V_DIM
D_FF = 4 * D_MODEL
ROPE_THETA = 10000.0
ALPHA = (2.0 * DEPTH) ** 0.25
LN_EPS = 1e-5
RMS_EPS = 1e-6

LANES = 128
SUBLANES = 8
MLA_KPAD = 256
IN_COLS_PAD = 3 * A_WIDTH + Q_LORA + KV_LORA + LANES
VMEM_LIMIT = 56 * 1024 * 1024

SPAN = 128
MAX_DILATION = 16
TOKEN_GROUP = SPAN
CHUNK_GROUPS = SPAN * MAX_DILATION // TOKEN_GROUP

F32 = jnp.float32
BF16 = jnp.bfloat16
NEG_INF = float("-inf")
LOG2E = 1.4426950408889634
MLA_Q_SCALE = (QK_NOPE + QK_ROPE) ** -0.5 * LOG2E
A_Q_SCALE = A_HEAD_DIM ** -0.5 * LOG2E


def _const_spec(shape):
    return pl.BlockSpec(shape, lambda *_: (0,) * len(shape), pipeline_mode=pl.Buffered(1))


def _params(n_axes, vmem_limit=VMEM_LIMIT):
    return pltpu.CompilerParams(dimension_semantics=("arbitrary",) * n_axes,
                                vmem_limit_bytes=vmem_limit)


def _rms(x, g):
    return x * lax.rsqrt(jnp.mean(x * x, -1, keepdims=True) + RMS_EPS) * g


def _layer_norm(z, g, b):
    mu = jnp.mean(z, -1, keepdims=True)
    zc = z - mu
    var = jnp.mean(zc * zc, -1, keepdims=True)
    return zc * lax.rsqrt(var + LN_EPS) * g + b


def _rope128(x, cos, sin_signed, first_half):
    rot = jnp.where(first_half, pltpu.roll(x, 96, 1), pltpu.roll(x, 32, 1))
    return x * cos + rot * sin_signed


def _position_in_group(i):
    return 16 * (i & 7) + 4 * ((i >> 3) & 3) + ((i >> 5) & 3)


def _in_proj_kernel(x_ref, w_in_ref, w_uq_ref, w_ukv_ref, gq_ref, gkv_ref, cos_ref, sin_ref,
                    qa_ref, ka_ref, va_ref, qm_ref, km_ref, vm_ref):
    h = jnp.dot(x_ref[...].astype(BF16), w_in_ref[...], preferred_element_type=F32)
    cos = cos_ref[...]
    sin = sin_ref[...]
    lane = lax.broadcasted_iota(jnp.int32, cos.shape, 1)
    first_half = (lane & (A_HEAD_DIM - 1)) < (A_HEAD_DIM // 2)

    def rope(t):
        return _rope128(t, cos, sin, first_half)

    def rope_cols(t):
        return jnp.concatenate(
            [rope(t[:, c * LANES:(c + 1) * LANES]) for c in range(t.shape[1] // LANES)], axis=1)

    qa_ref[...] = rope_cols(h[:, 0:A_WIDTH]) * A_Q_SCALE
    ka_ref[...] = rope_cols(h[:, A_WIDTH:2 * A_WIDTH])
    va_ref[...] = h[:, 2 * A_WIDTH:3 * A_WIDTH]

    c0 = 3 * A_WIDTH
    cq = _rms(h[:, c0:c0 + Q_LORA], gq_ref[...])
    q = jnp.dot(cq.astype(BF16), w_uq_ref[...], preferred_element_type=F32)
    q_parts = []
    for hh in range(MLA_HEADS):
        base = hh * MLA_KPAD
        q_parts.append(q[:, base:base + QK_NOPE])
        q_parts.append(rope(q[:, base + QK_NOPE:base + MLA_KPAD]))
    qm_ref[...] = (jnp.concatenate(q_parts, axis=1) * MLA_Q_SCALE).astype(BF16)

    c1 = c0 + Q_LORA
    ckv = _rms(h[:, c1:c1 + KV_LORA], gkv_ref[...])
    kv = jnp.dot(ckv.astype(BF16), w_ukv_ref[...], preferred_element_type=F32)
    c2 = c1 + KV_LORA
    k_pe = rope(h[:, c2:c2 + LANES])
    k_parts = []
    for hh in range(MLA_HEADS):
        k_parts.append(kv[:, hh * QK_NOPE:(hh + 1) * QK_NOPE])
        k_parts.append(k_pe)
    km_ref[...] = jnp.concatenate(k_parts, axis=1).astype(BF16)
    vm_ref[...] = kv[:, MLA_HEADS * QK_NOPE:].astype(BF16)


def _in_proj(x, w_in, w_uq, w_ukv, gq, gkv, cos, sin, *, seq, tm):
    t = x.shape[0]
    n_pos_blocks = seq // tm
    row = lambda i: (i, 0)
    pos = lambda i: (i % n_pos_blocks, 0)
    outs = ((A_WIDTH, F32), (A_WIDTH, F32), (A_WIDTH, F32),
            (MLA_HEADS * MLA_KPAD, BF16), (MLA_HEADS * MLA_KPAD, BF16), (MLA_WIDTH, BF16))
    return pl.pallas_call(
        _in_proj_kernel,
        grid=(t // tm,),
        in_specs=[
            pl.BlockSpec((tm, D_MODEL), row),
            _const_spec(w_in.shape), _const_spec(w_uq.shape), _const_spec(w_ukv.shape),
            _const_spec(gq.shape), _const_spec(gkv.shape),
            pl.BlockSpec((tm, LANES), pos), pl.BlockSpec((tm, LANES), pos),
        ],
        out_specs=[pl.BlockSpec((tm, w), row) for w, _ in outs],
        out_shape=[jax.ShapeDtypeStruct((t, w), dt) for w, dt in outs],
        compiler_params=_params(1),
        name="in_proj",
    )(x, w_in, w_uq, w_ukv, gq, gkv, cos, sin)


def _mla_kernel(q_ref, k_ref, v_ref, o_ref, m_sc, acc_sc, *, tk):
    qi = pl.program_id(1)
    m_sc[...] = jnp.full(m_sc.shape, NEG_INF, F32)
    acc_sc[...] = jnp.zeros(acc_sc.shape, F32)

    def tile(k_start, causal):
        for h in range(MLA_HEADS):
            q = q_ref[:, h * MLA_KPAD:(h + 1) * MLA_KPAD]
            k = k_ref[pl.ds(k_start, tk), h * MLA_KPAD:(h + 1) * MLA_KPAD]
            v = v_ref[pl.ds(k_start, tk), h * V_DIM:(h + 1) * V_DIM]
            s = lax.dot_general(q, k, (((1,), (1,)), ((), ())), preferred_element_type=F32)
            if causal is not None:
                s = jnp.where(causal, s, NEG_INF)
            m_prev = m_sc[h]
            m_new = jnp.maximum(m_prev, jnp.max(s, -1, keepdims=True))
            alpha = jnp.exp2(m_prev - m_new)
            p = jnp.exp2(s - jnp.concatenate([m_new] * (tk // LANES), axis=1))
            v1 = jnp.concatenate([v, jnp.ones_like(v)], axis=1)
            acc_sc[h] = (jnp.concatenate([alpha, alpha], axis=1) * acc_sc[h]
                         + jnp.dot(p.astype(BF16), v1, preferred_element_type=F32))
            m_sc[h] = m_new

    def body(ki, carry):
        tile(pl.multiple_of(ki * tk, tk), None)
        return carry

    lax.fori_loop(0, qi, body, 0)
    shape = (q_ref.shape[0], tk)
    r = lax.broadcasted_iota(jnp.int32, shape, 0)
    c = lax.broadcasted_iota(jnp.int32, shape, 1)
    r_pos = (r & -TOKEN_GROUP) + _position_in_group(r)
    c_pos = (c & -TOKEN_GROUP) + _position_in_group(c)
    tile(pl.multiple_of(qi * tk, tk), c_pos <= r_pos)
    for h in range(MLA_HEADS):
        acc = acc_sc[h]
        o_ref[:, h * V_DIM:(h + 1) * V_DIM] = acc[:, :V_DIM] / acc[:, V_DIM:]


def _mla_attention(qm, km, vm, *, batch, seq, tq):
    t = qm.shape[0]
    nq = seq // tq
    q_map = lambda b, qi: (b * nq + qi, 0)
    kv_map = lambda b, qi: (b, 0)
    return pl.pallas_call(
        functools.partial(_mla_kernel, tk=tq),
        grid=(batch, nq),
        in_specs=[
            pl.BlockSpec((tq, MLA_HEADS * MLA_KPAD), q_map),
            pl.BlockSpec((seq, MLA_HEADS * MLA_KPAD), kv_map),
            pl.BlockSpec((seq, MLA_WIDTH), kv_map),
        ],
        out_specs=pl.BlockSpec((tq, MLA_WIDTH), q_map),
        out_shape=jax.ShapeDtypeStruct((t, MLA_WIDTH), F32),
        scratch_shapes=[pltpu.VMEM((MLA_HEADS, tq, LANES), F32),
                        pltpu.VMEM((MLA_HEADS, tq, 2 * V_DIM), F32)],
        compiler_params=_params(2),
        name="mla_attention",
    )(qm, km, vm)


HEAD_PAIRS = A_WIDTH // LANES


def _sub_index(i, dilation):
    if dilation == 16:
        return i
    if dilation == 4:
        return 32 * (i >> 5) + 4 * (i & 7) + ((i >> 3) & 3)
    return _position_in_group(i)


def _band_masks(dilation):
    i = lax.broadcasted_iota(jnp.int32, (2 * SPAN, 2 * SPAN), 0) & (SPAN - 1)
    j = lax.broadcasted_iota(jnp.int32, (2 * SPAN, 2 * SPAN), 1)
    dist = SPAN + _sub_index(i, dilation) - ((j & SPAN) + _sub_index(j & (SPAN - 1), dilation))
    return (dist >= 0) & (dist <= SPAN), j >= SPAN


def _dilated_kernel(q_ref, k_ref, v_ref, o_ref, kp_sc, vp_sc, m_sc, l_sc):
    has_prev = pl.program_id(1) > 0
    lane = lax.broadcasted_iota(jnp.int32, (SPAN, LANES), 1)
    head_a = lane < A_HEAD_DIM
    ones = jnp.ones((2 * SPAN, LANES), BF16)

    def rows2d(t):
        return t.reshape(-1, LANES)

    def attend(q, k, v, mask, idx, first, last):
        qa = jnp.where(head_a, q, 0.0).astype(BF16)
        qb = jnp.where(head_a, 0.0, q).astype(BF16)
        q2 = jnp.concatenate([qa, qb], axis=0)
        s = lax.dot_general(q2, k.astype(BF16), (((1,), (1,)), ((), ())), preferred_element_type=F32)
        s = jnp.where(mask, s, NEG_INF)
        m = jnp.max(s, -1, keepdims=True)
        p = jnp.exp2(s - m)
        pv = jnp.dot(p.astype(BF16), jnp.concatenate([v.astype(BF16), ones], axis=1),
                     preferred_element_type=F32)
        acc = jnp.where(head_a, pv[:SPAN, :LANES], pv[SPAN:, :LANES])
        l = jnp.where(head_a, pv[:SPAN, LANES:], pv[SPAN:, LANES:])
        m_cur = jnp.where(head_a, jnp.broadcast_to(m[:SPAN], (SPAN, LANES)),
                          jnp.broadcast_to(m[SPAN:], (SPAN, LANES)))
        shape = o_ref[idx].shape
        if first:
            o_ref[idx] = acc.reshape(shape)
            l_sc[idx] = l.reshape(shape)
            m_sc[idx] = m_cur.reshape(shape)
            return
        m_old = rows2d(m_sc[idx])
        m_new = jnp.maximum(m_old, m_cur)
        a_old = jnp.exp2(m_old - m_new)
        a_cur = jnp.exp2(m_cur - m_new)
        acc = a_old * rows2d(o_ref[idx]) + a_cur * acc
        l = a_old * rows2d(l_sc[idx]) + a_cur * l
        if last:
            o_ref[idx] = (acc / l).reshape(shape)
        else:
            o_ref[idx] = acc.reshape(shape)
            l_sc[idx] = l.reshape(shape)
            m_sc[idx] = m_new.reshape(shape)

    every = slice(None)

    band, own = _band_masks(1)
    band_first = band & (own | has_prev)
    for p in range(HEAD_PAIRS):
        cols = pl.ds(p * LANES, LANES)
        k = jnp.concatenate([rows2d(kp_sc[CHUNK_GROUPS - 1, :, :, cols]), rows2d(k_ref[0, :, :, cols])], axis=0)
        v = jnp.concatenate([rows2d(vp_sc[CHUNK_GROUPS - 1, :, :, cols]), rows2d(v_ref[0, :, :, cols])], axis=0)
        attend(rows2d(q_ref[0, :, :, cols]), k, v, band_first, (0, every, every, cols), True, False)

    def d1_body(g, carry):
        for p in range(HEAD_PAIRS):
            cols = pl.ds(p * LANES, LANES)
            attend(rows2d(q_ref[g, :, :, cols]), rows2d(k_ref[pl.ds(g - 1, 2), :, :, cols]),
                   rows2d(v_ref[pl.ds(g - 1, 2), :, :, cols]), band, (g, every, every, cols), True, False)
        return carry
    lax.fori_loop(1, CHUNK_GROUPS, d1_body, 0)

    band, own = _band_masks(4)
    band_first = band & (own | has_prev)

    def d4_first(r4, carry):
        res = pl.ds(4 * r4, 4)
        for p in range(HEAD_PAIRS):
            cols = pl.ds(p * LANES, LANES)
            k = jnp.concatenate([rows2d(kp_sc[CHUNK_GROUPS - 4:, res, :, cols]), rows2d(k_ref[0:4, res, :, cols])],
                                axis=0)
            v = jnp.concatenate([rows2d(vp_sc[CHUNK_GROUPS - 4:, res, :, cols]), rows2d(v_ref[0:4, res, :, cols])],
                                axis=0)
            attend(rows2d(q_ref[0:4, res, :, cols]), k, v, band_first, (slice(0, 4), res, every, cols), False, False)
        return carry
    lax.fori_loop(0, 4, d4_first, 0)

    blocks_after_first = CHUNK_GROUPS // 4 - 1

    def d4_body(t, carry):
        res = pl.ds(4 * (t // blocks_after_first), 4)
        g0 = 4 * (1 + t % blocks_after_first)
        for p in range(HEAD_PAIRS):
            cols = pl.ds(p * LANES, LANES)
            attend(rows2d(q_ref[pl.ds(g0, 4), res, :, cols]), rows2d(k_ref[pl.ds(g0 - 4, 8), res, :, cols]),
                   rows2d(v_ref[pl.ds(g0 - 4, 8), res, :, cols]), band, (pl.ds(g0, 4), res, every, cols),
                   False, False)
        return carry
    lax.fori_loop(0, 4 * blocks_after_first, d4_body, 0)

    band, own = _band_masks(16)
    band16 = band & (own | has_prev)

    def d16_body(r, carry):
        for p in range(HEAD_PAIRS):
            cols = pl.ds(p * LANES, LANES)
            k = jnp.concatenate([rows2d(kp_sc[:, r, :, cols]), rows2d(k_ref[:, r, :, cols])], axis=0)
            v = jnp.concatenate([rows2d(vp_sc[:, r, :, cols]), rows2d(v_ref[:, r, :, cols])], axis=0)
            attend(rows2d(q_ref[:, r, :, cols]), k, v, band16, (every, r, every, cols), False, True)
        return carry
    lax.fori_loop(0, MAX_DILATION, d16_body, 0)

    kp_sc[...] = k_ref[...]
    vp_sc[...] = v_ref[...]


def _dilated_attention(qa, ka, va, *, batch, seq):
    t = qa.shape[0]
    chunks = seq // (CHUNK_GROUPS * TOKEN_GROUP)
    tiles = TOKEN_GROUP // SUBLANES
    shape4 = (t // TOKEN_GROUP, tiles, SUBLANES, A_WIDTH)
    view = lambda a: a.reshape(shape4)
    block = (CHUNK_GROUPS, tiles, SUBLANES, A_WIDTH)
    spec = pl.BlockSpec(block, lambda b, c: (b * chunks + c, 0, 0, 0))
    scratch = pltpu.VMEM(block, F32)
    out = pl.pallas_call(
        _dilated_kernel_entry,
        grid=(batch, chunks),
        in_specs=[spec, spec, spec],
        out_specs=spec,
        out_shape=jax.ShapeDtypeStruct(shape4, F32),
        scratch_shapes=[scratch, scratch, scratch, scratch],
        compiler_params=_params(2),
        name="dilated",
    )(view(qa), view(ka), view(va))
    return out.reshape(t, A_WIDTH)


def _dilated_kernel_entry(q_ref, k_ref, v_ref, o_ref, kp_sc, vp_sc, m_sc, l_sc):
    @pl.when((pl.program_id(0) == 0) & (pl.program_id(1) == 0))
    def _():
        kp_sc[...] = jnp.zeros(kp_sc.shape, F32)
        vp_sc[...] = jnp.zeros(vp_sc.shape, F32)

    _dilated_kernel(q_ref, k_ref, v_ref, o_ref, kp_sc, vp_sc, m_sc, l_sc)


def _out_proj_kernel(a_ref, b_ref, x_ref, wo_ref, ga_ref, gb_ref, g_ref, beta_ref, o_ref):
    mixed = jnp.concatenate([_rms(a_ref[...], ga_ref[...]), _rms(b_ref[...], gb_ref[...])], axis=1)
    y = jnp.dot(mixed.astype(BF16), wo_ref[...], preferred_element_type=F32)
    o_ref[...] = _layer_norm(ALPHA * x_ref[...] + y, g_ref[...], beta_ref[...])


def _out_proj(a, b, x, wo, ga, gb, g, beta, *, tm):
    t = x.shape[0]
    row = lambda i: (i, 0)
    return pl.pallas_call(
        _out_proj_kernel,
        grid=(t // tm,),
        in_specs=[pl.BlockSpec((tm, A_WIDTH), row), pl.BlockSpec((tm, MLA_WIDTH), row),
                  pl.BlockSpec((tm, D_MODEL), row), _const_spec(wo.shape),
                  _const_spec(ga.shape), _const_spec(gb.shape), _const_spec(g.shape),
                  _const_spec(beta.shape)],
        out_specs=pl.BlockSpec((tm, D_MODEL), row),
        out_shape=jax.ShapeDtypeStruct((t, D_MODEL), F32),
        compiler_params=_params(1),
        name="out_proj",
    )(a, b, x, wo, ga, gb, g, beta)


FF_CHUNK = 1024


def _ffn_kernel(x_ref, w1_ref, w2_ref, g_ref, beta_ref, o_ref):
    x = x_ref[...]
    xb = x.astype(BF16)
    acc = ALPHA * x
    for c in range(D_FF // FF_CHUNK):
        cols = slice(c * FF_CHUNK, (c + 1) * FF_CHUNK)
        hdn = jnp.dot(xb, w1_ref[:, cols], preferred_element_type=F32)
        hdn = jnp.square(jnp.maximum(hdn, 0.0)).astype(BF16)
        acc = acc + jnp.dot(hdn, w2_ref[cols, :], preferred_element_type=F32)
    o_ref[...] = _layer_norm(acc, g_ref[...], beta_ref[...])


def _ffn(x, w1, w2, g, beta, *, tm):
    t = x.shape[0]
    row = lambda i: (i, 0)
    return pl.pallas_call(
        _ffn_kernel,
        grid=(t // tm,),
        in_specs=[pl.BlockSpec((tm, D_MODEL), row), _const_spec(w1.shape), _const_spec(w2.shape),
                  _const_spec(g.shape), _const_spec(beta.shape)],
        out_specs=pl.BlockSpec((tm, D_MODEL), row),
        out_shape=jax.ShapeDtypeStruct((t, D_MODEL), F32),
        compiler_params=_params(1),
        name="ffn",
    )(x, w1, w2, g, beta)


def _permute_tokens(x, inverse=False):
    b, s, d = x.shape
    dims = (4, 4, 8) if inverse else (8, 4, 4)
    x = x.reshape((b, s // TOKEN_GROUP) + dims + (d,))
    return x.transpose(0, 1, 4, 3, 2, 5).reshape(b, s, d)


def _rope_tables(seq):
    half = A_HEAD_DIM // 2
    inv_freq = ROPE_THETA ** (-jnp.arange(half, dtype=F32) / half)
    row = jnp.arange(seq, dtype=jnp.int32)
    pos = (row & -TOKEN_GROUP) + _position_in_group(row)
    ang = pos.astype(F32)[:, None] * inv_freq[None, :]
    cos = jnp.cos(ang)
    sin = jnp.sin(ang)
    cos128 = jnp.tile(cos, (1, LANES // half))
    sin128 = jnp.tile(jnp.concatenate([-sin, sin], axis=1), (1, LANES // A_HEAD_DIM))
    return cos128, sin128


def _prep_w_uq(w):
    w = w.reshape(Q_LORA, MLA_HEADS, QK_NOPE + QK_ROPE)
    w = jnp.pad(w, ((0, 0), (0, 0), (0, MLA_KPAD - QK_NOPE - QK_ROPE)))
    return w.reshape(Q_LORA, MLA_HEADS * MLA_KPAD).astype(BF16)


def _prep_w_ukv(w):
    w = w.reshape(KV_LORA, MLA_HEADS, QK_NOPE + V_DIM)
    k = w[:, :, :QK_NOPE].reshape(KV_LORA, MLA_HEADS * QK_NOPE)
    v = w[:, :, QK_NOPE:].reshape(KV_LORA, MLA_HEADS * V_DIM)
    return jnp.concatenate([k, v], axis=1).astype(BF16)


def kernel(x, w_in, q_a_norm, kv_a_norm, w_uq, w_ukv, a_out_norm, b_out_norm, w_o,
           ln1_g, ln1_b, w_ff1, w_ff2, ln2_g, ln2_b):
    batch, seq, _ = x.shape
    assert seq % (CHUNK_GROUPS * TOKEN_GROUP) == 0
    t = batch * seq
    cos, sin = _rope_tables(seq)
    xf = _permute_tokens(x).reshape(t, D_MODEL)
    row2d = lambda v: v.reshape(1, -1)
    for l in range(DEPTH):
        w_in_l = jnp.pad(w_in[l], ((0, 0), (0, IN_COLS_PAD - w_in.shape[2]))).astype(BF16)
        qa, ka, va, qm, km, vm = _in_proj(
            xf, w_in_l, _prep_w_uq(w_uq[l]), _prep_w_ukv(w_ukv[l]),
            row2d(q_a_norm[l]), row2d(kv_a_norm[l]), cos, sin, seq=seq, tm=512)
        b_out = _mla_attention(qm, km, vm, batch=batch, seq=seq, tq=512)
        a_out = _dilated_attention(qa, ka, va, batch=batch, seq=seq)
        xf = _out_proj(a_out, b_out, xf, w_o[l].astype(BF16), row2d(a_out_norm[l]),
                       row2d(b_out_norm[l]), row2d(ln1_g[l]), row2d(ln1_b[l]), tm=512)
        xf = _ffn(xf, w_ff1[l].astype(BF16), w_ff2[l].astype(BF16), row2d(ln2_g[l]),
                  row2d(ln2_b[l]), tm=512)
    return _permute_tokens(xf.reshape(batch, seq, D_MODEL), inverse=True)
```

```python
import functools

import jax
import jax.numpy as jnp
from jax import lax
from jax.experimental import pallas as pl
from jax.experimental.pallas import tpu as pltpu

D_MODEL = 1024
DEPTH = 4
A_HEADS = 8
A_HEAD_DIM = 64
A_WIDTH = A_HEADS * A_HEAD_DIM
MLA_HEADS = 4
QK_NOPE = 128
QK_ROPE = 64
V_DIM = 128
Q_LORA = 256
KV_LORA = 128
MLA_WIDTH = MLA_HEADS * V_DIM
D_FF = 4 * D_MODEL
ROPE_THETA = 10000.0
ALPHA = (2.0 * DEPTH) ** 0.25
LN_EPS = 1e-5
RMS_EPS = 1e-6

LANES = 128
SUBLANES = 8
MLA_KPAD = 256
IN_COLS_PAD = 3 * A_WIDTH + Q_LORA + KV_LORA + LANES
VMEM_LIMIT = 56 * 1024 * 1024

SPAN = 128
MAX_DILATION = 16
TOKEN_GROUP = SPAN
CHUNK_GROUPS = SPAN * MAX_DILATION // TOKEN_GROUP

F32 = jnp.float32
BF16 = jnp.bfloat16
NEG_INF = float("-inf")
LOG2E = 1.4426950408889634
MLA_Q_SCALE = (QK_NOPE + QK_ROPE) ** -0.5 * LOG2E
A_Q_SCALE = A_HEAD_DIM ** -0.5 * LOG2E


def _const_spec(shape):
    return pl.BlockSpec(shape, lambda *_: (0,) * len(shape), pipeline_mode=pl.Buffered(1))


def _params(n_axes, vmem_limit=VMEM_LIMIT):
    return pltpu.CompilerParams(dimension_semantics=("arbitrary",) * n_axes,
                                vmem_limit_bytes=vmem_limit)


def _rms(x, g):
    return x * lax.rsqrt(jnp.mean(x * x, -1, keepdims=True) + RMS_EPS) * g


def _layer_norm(z, g, b):
    mu = jnp.mean(z, -1, keepdims=True)
    zc = z - mu
    var = jnp.mean(zc * zc, -1, keepdims=True)
    return zc * lax.rsqrt(var + LN_EPS) * g + b


def _rope128(x, cos, sin_signed, first_half):
    rot = jnp.where(first_half, pltpu.roll(x, 96, 1), pltpu.roll(x, 32, 1))
    return x * cos + rot * sin_signed


def _position_in_group(i):
    return 16 * (i & 7) + 4 * ((i >> 3) & 3) + ((i >> 5) & 3)


def _in_proj_kernel(x_ref, w_in_ref, w_uq_ref, w_ukv_ref, gq_ref, gkv_ref, cos_ref, sin_ref,
                    qa_ref, ka_ref, va_ref, qm_ref, km_ref, vm_ref):
    h = jnp.dot(x_ref[...].astype(BF16), w_in_ref[...], preferred_element_type=F32)
    cos = cos_ref[...]
    sin = sin_ref[...]
    lane = lax.broadcasted_iota(jnp.int32, cos.shape, 1)
    first_half = (lane & (A_HEAD_DIM - 1)) < (A_HEAD_DIM // 2)

    def rope(t):
        return _rope128(t, cos, sin, first_half)

    def rope_cols(t):
        return jnp.concatenate(
            [rope(t[:, c * LANES:(c + 1) * LANES]) for c in range(t.shape[1] // LANES)], axis=1)

    qa_ref[...] = rope_cols(h[:, 0:A_WIDTH]) * A_Q_SCALE
    ka_ref[...] = rope_cols(h[:, A_WIDTH:2 * A_WIDTH])
    va_ref[...] = h[:, 2 * A_WIDTH:3 * A_WIDTH]

    c0 = 3 * A_WIDTH
    cq = _rms(h[:, c0:c0 + Q_LORA], gq_ref[...])
    q = jnp.dot(cq.astype(BF16), w_uq_ref[...], preferred_element_type=F32)
    q_parts = []
    for hh in range(MLA_HEADS):
        base = hh * MLA_KPAD
        q_parts.append(q[:, base:base + QK_NOPE])
        q_parts.append(rope(q[:, base + QK_NOPE:base + MLA_KPAD]))
    qm_ref[...] = (jnp.concatenate(q_parts, axis=1) * MLA_Q_SCALE).astype(BF16)

    c1 = c0 + Q_LORA
    ckv = _rms(h[:, c1:c1 + KV_LORA], gkv_ref[...])
    kv = jnp.dot(ckv.astype(BF16), w_ukv_ref[...], preferred_element_type=F32)
    c2 = c1 + KV_LORA
    k_pe = rope(h[:, c2:c2 + LANES])
    k_parts = []
    for hh in range(MLA_HEADS):
        k_parts.append(kv[:, hh * QK_NOPE:(hh + 1) * QK_NOPE])
        k_parts.append(k_pe)
    km_ref[...] = jnp.concatenate(k_parts, axis=1).astype(BF16)
    vm_ref[...] = kv[:, MLA_HEADS * QK_NOPE:].astype(BF16)


def _in_proj(x, w_in, w_uq, w_ukv, gq, gkv, cos, sin, *, seq, tm):
    t = x.shape[0]
    n_pos_blocks = seq // tm
    row = lambda i: (i, 0)
    pos = lambda i: (i % n_pos_blocks, 0)
    outs = ((A_WIDTH, F32), (A_WIDTH, F32), (A_WIDTH, F32),
            (MLA_HEADS * MLA_KPAD, BF16), (MLA_HEADS * MLA_KPAD, BF16), (MLA_WIDTH, BF16))
    return pl.pallas_call(
        _in_proj_kernel,
        grid=(t // tm,),
        in_specs=[
            pl.BlockSpec((tm, D_MODEL), row),
            _const_spec(w_in.shape), _const_spec(w_uq.shape), _const_spec(w_ukv.shape),
            _const_spec(gq.shape), _const_spec(gkv.shape),
            pl.BlockSpec((tm, LANES), pos), pl.BlockSpec((tm, LANES), pos),
        ],
        out_specs=[pl.BlockSpec((tm, w), row) for w, _ in outs],
        out_shape=[jax.ShapeDtypeStruct((t, w), dt) for w, dt in outs],
        compiler_params=_params(1),
        name="in_proj",
    )(x, w_in, w_uq, w_ukv, gq, gkv, cos, sin)


def _mla_kernel(q_ref, k_ref, v_ref, o_ref, m_sc, acc_sc, *, tk):
    qi = pl.program_id(1)
    m_sc[...] = jnp.full(m_sc.shape, NEG_INF, F32)
    acc_sc[...] = jnp.zeros(acc_sc.shape, F32)

    def tile(k_start, causal):
        for h in range(MLA_HEADS):
            q = q_ref[:, h * MLA_KPAD:(h + 1) * MLA_KPAD]
            k = k_ref[pl.ds(k_start, tk), h * MLA_KPAD:(h + 1) * MLA_KPAD]
            v = v_ref[pl.ds(k_start, tk), h * V_DIM:(h + 1) * V_DIM]
            s = lax.dot_general(q, k, (((1,), (1,)), ((), ())), preferred_element_type=F32)
            if causal is not None:
                s = jnp.where(causal, s, NEG_INF)
            m_prev = m_sc[h]
            m_new = jnp.maximum(m_prev, jnp.max(s, -1, keepdims=True))
            alpha = jnp.exp2(m_prev - m_new)
            p = jnp.exp2(s - jnp.concatenate([m_new] * (tk // LANES), axis=1))
            v1 = jnp.concatenate([v, jnp.ones_like(v)], axis=1)
            acc_sc[h] = (jnp.concatenate([alpha, alpha], axis=1) * acc_sc[h]
                         + jnp.dot(p.astype(BF16), v1, preferred_element_type=F32))
            m_sc[h] = m_new

    def body(ki, carry):
        tile(pl.multiple_of(ki * tk, tk), None)
        return carry

    lax.fori_loop(0, qi, body, 0)
    shape = (q_ref.shape[0], tk)
    r = lax.broadcasted_iota(jnp.int32, shape, 0)
    c = lax.broadcasted_iota(jnp.int32, shape, 1)
    r_pos = (r & -TOKEN_GROUP) + _position_in_group(r)
    c_pos = (c & -TOKEN_GROUP) + _position_in_group(c)
    tile(pl.multiple_of(qi * tk, tk), c_pos <= r_pos)
    for h in range(MLA_HEADS):
        acc = acc_sc[h]
        o_ref[:, h * V_DIM:(h + 1) * V_DIM] = acc[:, :V_DIM] / acc[:, V_DIM:]


def _mla_attention(qm, km, vm, *, batch, seq, tq):
    t = qm.shape[0]
    nq = seq // tq
    q_map = lambda b, qi: (b * nq + qi, 0)
    kv_map = lambda b, qi: (b, 0)
    return pl.pallas_call(
        functools.partial(_mla_kernel, tk=tq),
        grid=(batch, nq),
        in_specs=[
            pl.BlockSpec((tq, MLA_HEADS * MLA_KPAD), q_map),
            pl.BlockSpec((seq, MLA_HEADS * MLA_KPAD), kv_map),
            pl.BlockSpec((seq, MLA_WIDTH), kv_map),
        ],
        out_specs=pl.BlockSpec((tq, MLA_WIDTH), q_map),
        out_shape=jax.ShapeDtypeStruct((t, MLA_WIDTH), F32),
        scratch_shapes=[pltpu.VMEM((MLA_HEADS, tq, LANES), F32),
                        pltpu.VMEM((MLA_HEADS, tq, 2 * V_DIM), F32)],
        compiler_params=_params(2),
        name="mla_attention",
    )(qm, km, vm)


HEAD_PAIRS = A_WIDTH // LANES


def _sub_index(i, dilation):
    if dilation == 16:
        return i
    if dilation == 4:
        return 32 * (i >> 5) + 4 * (i & 7) + ((i >> 3) & 3)
    return _position_in_group(i)


def _band_masks(dilation):
    i = lax.broadcasted_iota(jnp.int32, (2 * SPAN, 2 * SPAN), 0) & (SPAN - 1)
    j = lax.broadcasted_iota(jnp.int32, (2 * SPAN, 2 * SPAN), 1)
    dist = SPAN + _sub_index(i, dilation) - ((j & SPAN) + _sub_index(j & (SPAN - 1), dilation))
    return (dist >= 0) & (dist <= SPAN), j >= SPAN


def _dilated_kernel(q_ref, k_ref, v_ref, o_ref, kp_sc, vp_sc, m_sc, l_sc):
    has_prev = pl.program_id(1) > 0
    lane = lax.broadcasted_iota(jnp.int32, (SPAN, LANES), 1)
    head_a = lane < A_HEAD_DIM
    ones = jnp.ones((2 * SPAN, LANES), BF16)

    def rows2d(t):
        return t.reshape(-1, LANES)

    def attend(q, k, v, mask, idx, first, last):
        qa = jnp.where(head_a, q, 0.0).astype(BF16)
        qb = jnp.where(head_a, 0.0, q).astype(BF16)
        q2 = jnp.concatenate([qa, qb], axis=0)
        s = lax.dot_general(q2, k.astype(BF16), (((1,), (1,)), ((), ())), preferred_element_type=F32)
        s = jnp.where(mask, s, NEG_INF)
        m = jnp.max(s, -1, keepdims=True)
        p = jnp.exp2(s - m)
        pv = jnp.dot(p.astype(BF16), jnp.concatenate([v.astype(BF16), ones], axis=1),
                     preferred_element_type=F32)
        acc = jnp.where(head_a, pv[:SPAN, :LANES], pv[SPAN:, :LANES])
        l = jnp.where(head_a, pv[:SPAN, LANES:], pv[SPAN:, LANES:])
        m_cur = jnp.where(head_a, jnp.broadcast_to(m[:SPAN], (SPAN, LANES)),
                          jnp.broadcast_to(m[SPAN:], (SPAN, LANES)))
        shape = o_ref[idx].shape
        if first:
            o_ref[idx] = acc.reshape(shape)
            l_sc[idx] = l.reshape(shape)
            m_sc[idx] = m_cur.reshape(shape)
            return
        m_old = rows2d(m_sc[idx])
        m_new = jnp.maximum(m_old, m_cur)
        a_old = jnp.exp2(m_old - m_new)
        a_cur = jnp.exp2(m_cur - m_new)
        acc = a_old * rows2d(o_ref[idx]) + a_cur * acc
        l = a_old * rows2d(l_sc[idx]) + a_cur * l
        if last:
            o_ref[idx] = (acc / l).reshape(shape)
        else:
            o_ref[idx] = acc.reshape(shape)
            l_sc[idx] = l.reshape(shape)
            m_sc[idx] = m_new.reshape(shape)

    every = slice(None)

    band, own = _band_masks(1)
    band_first = band & (own | has_prev)
    for p in range(HEAD_PAIRS):
        cols = pl.ds(p * LANES, LANES)
        k = jnp.concatenate([rows2d(kp_sc[CHUNK_GROUPS - 1, :, :, cols]), rows2d(k_ref[0, :, :, cols])], axis=0)
        v = jnp.concatenate([rows2d(vp_sc[CHUNK_GROUPS - 1, :, :, cols]), rows2d(v_ref[0, :, :, cols])], axis=0)
        attend(rows2d(q_ref[0, :, :, cols]), k, v, band_first, (0, every, every, cols), True, False)

    def d1_body(g, carry):
        for p in range(HEAD_PAIRS):
            cols = pl.ds(p * LANES, LANES)
            attend(rows2d(q_ref[g, :, :, cols]), rows2d(k_ref[pl.ds(g - 1, 2), :, :, cols]),
                   rows2d(v_ref[pl.ds(g - 1, 2), :, :, cols]), band, (g, every, every, cols), True, False)
        return carry
    lax.fori_loop(1, CHUNK_GROUPS, d1_body, 0)

    band, own = _band_masks(4)
    band_first = band & (own | has_prev)

    def d4_first(r4, carry):
        res = pl.ds(4 * r4, 4)
        for p in range(HEAD_PAIRS):
            cols = pl.ds(p * LANES, LANES)
            k = jnp.concatenate([rows2d(kp_sc[CHUNK_GROUPS - 4:, res, :, cols]), rows2d(k_ref[0:4, res, :, cols])],
                                axis=0)
            v = jnp.concatenate([rows2d(vp_sc[CHUNK_GROUPS - 4:, res, :, cols]), rows2d(v_ref[0:4, res, :, cols])],
                                axis=0)
            attend(rows2d(q_ref[0:4, res, :, cols]), k, v, band_first, (slice(0, 4), res, every, cols), False, False)
        return carry
    lax.fori_loop(0, 4, d4_first, 0)

    blocks_after_first = CHUNK_GROUPS // 4 - 1

    def d4_body(t, carry):
        res = pl.ds(4 * (t // blocks_after_first), 4)
        g0 = 4 * (1 + t % blocks_after_first)
        for p in range(HEAD_PAIRS):
            cols = pl.ds(p * LANES, LANES)
            attend(rows2d(q_ref[pl.ds(g0, 4), res, :, cols]), rows2d(k_ref[pl.ds(g0 - 4, 8), res, :, cols]),
                   rows2d(v_ref[pl.ds(g0 - 4, 8), res, :, cols]), band, (pl.ds(g0, 4), res, every, cols),
                   False, False)
        return carry
    lax.fori_loop(0, 4 * blocks_after_first, d4_body, 0)

    band, own = _band_masks(16)
    band16 = band & (own | has_prev)

    def d16_body(r, carry):
        for p in range(HEAD_PAIRS):
            cols = pl.ds(p * LANES, LANES)
            k = jnp.concatenate([rows2d(kp_sc[:, r, :, cols]), rows2d(k_ref[:, r, :, cols])], axis=0)
            v = jnp.concatenate([rows2d(vp_sc[:, r, :, cols]), rows2d(v_ref[:, r, :, cols])], axis=0)
            attend(rows2d(q_ref[:, r, :, cols]), k, v, band16, (every, r, every, cols), False, True)
        return carry
    lax.fori_loop(0, MAX_DILATION, d16_body, 0)

    kp_sc[...] = k_ref[...]
    vp_sc[...] = v_ref[...]


def _dilated_attention(qa, ka, va, *, batch, seq):
    t = qa.shape[0]
    chunks = seq // (CHUNK_GROUPS * TOKEN_GROUP)
    tiles = TOKEN_GROUP // SUBLANES
    shape4 = (t // TOKEN_GROUP, tiles, SUBLANES, A_WIDTH)
    view = lambda a: a.reshape(shape4)
    block = (CHUNK_GROUPS, tiles, SUBLANES, A_WIDTH)
    spec = pl.BlockSpec(block, lambda b, c: (b * chunks + c, 0, 0, 0))
    scratch = pltpu.VMEM(block, F32)
    out = pl.pallas_call(
        _dilated_kernel_entry,
        grid=(batch, chunks),
        in_specs=[spec, spec, spec],
        out_specs=spec,
        out_shape=jax.ShapeDtypeStruct(shape4, F32),
        scratch_shapes=[scratch, scratch, scratch, scratch],
        compiler_params=_params(2),
        name="dilated",
    )(view(qa), view(ka), view(va))
    return out.reshape(t, A_WIDTH)


def _dilated_kernel_entry(q_ref, k_ref, v_ref, o_ref, kp_sc, vp_sc, m_sc, l_sc):
    @pl.when((pl.program_id(0) == 0) & (pl.program_id(1) == 0))
    def _():
        kp_sc[...] = jnp.zeros(kp_sc.shape, F32)
        vp_sc[...] = jnp.zeros(vp_sc.shape, F32)

    _dilated_kernel(q_ref, k_ref, v_ref, o_ref, kp_sc, vp_sc, m_sc, l_sc)


FF_CHUNK = 1024


def _mix_ffn_kernel(a_ref, b_ref, x_ref, wo_ref, ga_ref, gb_ref, g1_ref, beta1_ref,
                    w1_ref, w2_ref, g2_ref, beta2_ref, o_ref):
    mixed = jnp.concatenate([_rms(a_ref[...], ga_ref[...]), _rms(b_ref[...], gb_ref[...])], axis=1)
    y = jnp.dot(mixed.astype(BF16), wo_ref[...], preferred_element_type=F32)
    x1 = _layer_norm(ALPHA * x_ref[...] + y, g1_ref[...], beta1_ref[...])
    xb = x1.astype(BF16)
    acc = ALPHA * x1
    for c in range(D_FF // FF_CHUNK):
        cols = slice(c * FF_CHUNK, (c + 1) * FF_CHUNK)
        hdn = jnp.dot(xb, w1_ref[:, cols], preferred_element_type=F32)
        hdn = jnp.square(jnp.maximum(hdn, 0.0)).astype(BF16)
        acc = acc + jnp.dot(hdn, w2_ref[cols, :], preferred_element_type=F32)
    o_ref[...] = _layer_norm(acc, g2_ref[...], beta2_ref[...])


def _mix_ffn(a, b, x, wo, ga, gb, g1, beta1, w1, w2, g2, beta2, *, tm):
    t = x.shape[0]
    row = lambda i: (i, 0)
    consts = (wo, ga, gb, g1, beta1, w1, w2, g2, beta2)
    return pl.pallas_call(
        _mix_ffn_kernel,
        grid=(t // tm,),
        in_specs=[pl.BlockSpec((tm, A_WIDTH), row), pl.BlockSpec((tm, MLA_WIDTH), row),
                  pl.BlockSpec((tm, D_MODEL), row)] + [_const_spec(c.shape) for c in consts],
        out_specs=pl.BlockSpec((tm, D_MODEL), row),
        out_shape=jax.ShapeDtypeStruct((t, D_MODEL), F32),
        compiler_params=_params(1),
        name="mix_ffn",
    )(a, b, x, *consts)


def _permute_kernel(x_ref, o_ref, slab_sc, *, inverse):
    tm, width = x_ref.shape
    tiles = TOKEN_GROUP // SUBLANES
    for g in range(tm // TOKEN_GROUP):
        for tile in range(tiles):
            first_token = 4 * (tile & 3) + (tile >> 2)
            natural = pl.ds(g * TOKEN_GROUP + first_token, SUBLANES, stride=tiles)
            permuted = pl.ds(g * TOKEN_GROUP + tile * SUBLANES, SUBLANES)
            for c in range(width // LANES):
                cols = pl.ds(c * LANES, LANES)
                if inverse:
                    slab_sc[c, natural, :] = x_ref[permuted, cols]
                else:
                    if g == 0 and tile == 0:
                        slab_sc[c] = x_ref[:, cols]
                    o_ref[permuted, cols] = slab_sc[c, natural, :]
    if inverse:
        for c in range(width // LANES):
            o_ref[:, pl.ds(c * LANES, LANES)] = slab_sc[c]


def _permute_tokens(x, inverse=False, *, tm=512):
    t, d = x.shape
    row = lambda i: (i, 0)
    return pl.pallas_call(
        functools.partial(_permute_kernel, inverse=inverse),
        grid=(t // tm,),
        in_specs=[pl.BlockSpec((tm, d), row)],
        out_specs=pl.BlockSpec((tm, d), row),
        out_shape=jax.ShapeDtypeStruct((t, d), x.dtype),
        scratch_shapes=[pltpu.VMEM((d // LANES, tm, LANES), x.dtype)],
        compiler_params=_params(1),
        name="unpermute" if inverse else "permute",
    )(x)


def _rope_tables(seq):
    half = A_HEAD_DIM // 2
    inv_freq = ROPE_THETA ** (-jnp.arange(half, dtype=F32) / half)
    row = jnp.arange(seq, dtype=jnp.int32)
    pos = (row & -TOKEN_GROUP) + _position_in_group(row)
    ang = pos.astype(F32)[:, None] * inv_freq[None, :]
    cos = jnp.cos(ang)
    sin = jnp.sin(ang)
    cos128 = jnp.tile(cos, (1, LANES // half))
    sin128 = jnp.tile(jnp.concatenate([-sin, sin], axis=1), (1, LANES // A_HEAD_DIM))
    return cos128, sin128


def _prep_w_uq(w):
    w = w.reshape(Q_LORA, MLA_HEADS, QK_NOPE + QK_ROPE)
    w = jnp.pad(w, ((0, 0), (0, 0), (0, MLA_KPAD - QK_NOPE - QK_ROPE)))
    return w.reshape(Q_LORA, MLA_HEADS * MLA_KPAD).astype(BF16)


def _prep_w_ukv(w):
    w = w.reshape(KV_LORA, MLA_HEADS, QK_NOPE + V_DIM)
    k = w[:, :, :QK_NOPE].reshape(KV_LORA, MLA_HEADS * QK_NOPE)
    v = w[:, :, QK_NOPE:].reshape(KV_LORA, MLA_HEADS * V_DIM)
    return jnp.concatenate([k, v], axis=1).astype(BF16)


def kernel(x, w_in, q_a_norm, kv_a_norm, w_uq, w_ukv, a_out_norm, b_out_norm, w_o,
           ln1_g, ln1_b, w_ff1, w_ff2, ln2_g, ln2_b):
    batch, seq, _ = x.shape
    assert seq % (CHUNK_GROUPS * TOKEN_GROUP) == 0
    t = batch * seq
    cos, sin = _rope_tables(seq)
    xf = _permute_tokens(x.reshape(t, D_MODEL))
    row2d = lambda v: v.reshape(1, -1)
    for l in range(DEPTH):
        w_in_l = jnp.pad(w_in[l], ((0, 0), (0, IN_COLS_PAD - w_in.shape[2]))).astype(BF16)
        qa, ka, va, qm, km, vm = _in_proj(
            xf, w_in_l, _prep_w_uq(w_uq[l]), _prep_w_ukv(w_ukv[l]),
            row2d(q_a_norm[l]), row2d(kv_a_norm[l]), cos, sin, seq=seq, tm=512)
        b_out = _mla_attention(qm, km, vm, batch=batch, seq=seq, tq=512)
        a_out = _dilated_attention(qa, ka, va, batch=batch, seq=seq)
        xf = _mix_ffn(a_out, b_out, xf, w_o[l].astype(BF16), row2d(a_out_norm[l]), row2d(b_out_norm[l]),
                      row2d(ln1_g[l]), row2d(ln1_b[l]), w_ff1[l].astype(BF16), w_ff2[l].astype(BF16),
                      row2d(ln2_g[l]), row2d(ln2_b[l]), tm=512)
    return _permute_tokens(xf, inverse=True).reshape(batch, seq, D_MODEL)
```

```python
import functools

import jax
import jax.numpy as jnp
from jax import lax
from jax.experimental import pallas as pl
from jax.experimental.pallas import tpu as pltpu

D_MODEL = 1024
DEPTH = 4
A_HEADS = 8
A_HEAD_DIM = 64
A_WIDTH = A_HEADS * A_HEAD_DIM
MLA_HEADS = 4
QK_NOPE = 128
QK_ROPE = 64
V_DIM = 128
Q_LORA = 256
KV_LORA = 128
MLA_WIDTH = MLA_HEADS * V_DIM
D_FF = 4 * D_MODEL
ROPE_THETA = 10000.0
ALPHA = (2.0 * DEPTH) ** 0.25
LN_EPS = 1e-5
RMS_EPS = 1e-6

LANES = 128
SUBLANES = 8
MLA_KPAD = 256
IN_COLS_PAD = 3 * A_WIDTH + Q_LORA + KV_LORA + LANES
VMEM_LIMIT = 56 * 1024 * 1024

SPAN = 128
MAX_DILATION = 16
TOKEN_GROUP = SPAN
CHUNK_GROUPS = SPAN * MAX_DILATION // TOKEN_GROUP

F32 = jnp.float32
BF16 = jnp.bfloat16
NEG_INF = float("-inf")
LOG2E = 1.4426950408889634
MLA_Q_SCALE = (QK_NOPE + QK_ROPE) ** -0.5 * LOG2E
A_Q_SCALE = A_HEAD_DIM ** -0.5 * LOG2E


def _const_spec(shape):
    return pl.BlockSpec(shape, lambda *_: (0,) * len(shape), pipeline_mode=pl.Buffered(1))


def _params(n_axes, vmem_limit=VMEM_LIMIT):
    return pltpu.CompilerParams(dimension_semantics=("arbitrary",) * n_axes,
                                vmem_limit_bytes=vmem_limit)


def _rms(x, g):
    return x * lax.rsqrt(jnp.mean(x * x, -1, keepdims=True) + RMS_EPS) * g


def _layer_norm(z, g, b):
    mu = jnp.mean(z, -1, keepdims=True)
    zc = z - mu
    var = jnp.mean(zc * zc, -1, keepdims=True)
    return zc * lax.rsqrt(var + LN_EPS) * g + b


def _rope128(x, cos, sin_signed, first_half):
    rot = jnp.where(first_half, pltpu.roll(x, 96, 1), pltpu.roll(x, 32, 1))
    return x * cos + rot * sin_signed


def _position_in_group(i):
    return 16 * (i & 7) + 4 * ((i >> 3) & 3) + ((i >> 5) & 3)


def _in_proj_kernel(x_ref, w_in_ref, w_uq_ref, w_ukv_ref, gq_ref, gkv_ref, cos_ref, sin_ref,
                    qa_ref, ka_ref, va_ref, qm_ref, km_ref, vm_ref):
    h = jnp.dot(x_ref[...].astype(BF16), w_in_ref[...], preferred_element_type=F32)
    cos = cos_ref[...]
    sin = sin_ref[...]
    lane = lax.broadcasted_iota(jnp.int32, cos.shape, 1)
    first_half = (lane & (A_HEAD_DIM - 1)) < (A_HEAD_DIM // 2)

    def rope(t):
        return _rope128(t, cos, sin, first_half)

    def rope_cols(t):
        return jnp.concatenate(
            [rope(t[:, c * LANES:(c + 1) * LANES]) for c in range(t.shape[1] // LANES)], axis=1)

    qa_ref[...] = rope_cols(h[:, 0:A_WIDTH]) * A_Q_SCALE
    ka_ref[...] = rope_cols(h[:, A_WIDTH:2 * A_WIDTH])
    va_ref[...] = h[:, 2 * A_WIDTH:3 * A_WIDTH]

    c0 = 3 * A_WIDTH
    cq = _rms(h[:, c0:c0 + Q_LORA], gq_ref[...])
    q = jnp.dot(cq.astype(BF16), w_uq_ref[...], preferred_element_type=F32)
    q_parts = []
    for hh in range(MLA_HEADS):
        base = hh * MLA_KPAD
        q_parts.append(q[:, base:base + QK_NOPE])
        q_parts.append(rope(q[:, base + QK_NOPE:base + MLA_KPAD]))
    qm_ref[...] = (jnp.concatenate(q_parts, axis=1) * MLA_Q_SCALE).astype(BF16)

    c1 = c0 + Q_LORA
    ckv = _rms(h[:, c1:c1 + KV_LORA], gkv_ref[...])
    kv = jnp.dot(ckv.astype(BF16), w_ukv_ref[...], preferred_element_type=F32)
    c2 = c1 + KV_LORA
    k_pe = rope(h[:, c2:c2 + LANES])
    k_parts = []
    for hh in range(MLA_HEADS):
        k_parts.append(kv[:, hh * QK_NOPE:(hh + 1) * QK_NOPE])
        k_parts.append(k_pe)
    km_ref[...] = jnp.concatenate(k_parts, axis=1).astype(BF16)
    vm_ref[...] = kv[:, MLA_HEADS * QK_NOPE:].astype(BF16)


def _in_proj(x, w_in, w_uq, w_ukv, gq, gkv, cos, sin, *, seq, tm):
    t = x.shape[0]
    n_pos_blocks = seq // tm
    row = lambda i: (i, 0)
    pos = lambda i: (i % n_pos_blocks, 0)
    outs = ((A_WIDTH, F32), (A_WIDTH, F32), (A_WIDTH, F32),
            (MLA_HEADS * MLA_KPAD, BF16), (MLA_HEADS * MLA_KPAD, BF16), (MLA_WIDTH, BF16))
    return pl.pallas_call(
        _in_proj_kernel,
        grid=(t // tm,),
        in_specs=[
            pl.BlockSpec((tm, D_MODEL), row),
            _const_spec(w_in.shape), _const_spec(w_uq.shape), _const_spec(w_ukv.shape),
            _const_spec(gq.shape), _const_spec(gkv.shape),
            pl.BlockSpec((tm, LANES), pos), pl.BlockSpec((tm, LANES), pos),
        ],
        out_specs=[pl.BlockSpec((tm, w), row) for w, _ in outs],
        out_shape=[jax.ShapeDtypeStruct((t, w), dt) for w, dt in outs],
        compiler_params=_params(1),
        name="in_proj",
    )(x, w_in, w_uq, w_ukv, gq, gkv, cos, sin)


def _mla_kernel(q_ref, k_ref, v_ref, o_ref, m_sc, acc_sc, *, tk):
    qi = pl.program_id(1)
    m_sc[...] = jnp.full(m_sc.shape, NEG_INF, F32)
    acc_sc[...] = jnp.zeros(acc_sc.shape, F32)

    def tile(k_start, causal):
        for h in range(MLA_HEADS):
            q = q_ref[:, h * MLA_KPAD:(h + 1) * MLA_KPAD]
            k = k_ref[pl.ds(k_start, tk), h * MLA_KPAD:(h + 1) * MLA_KPAD]
            v = v_ref[pl.ds(k_start, tk), h * V_DIM:(h + 1) * V_DIM]
            s = lax.dot_general(q, k, (((1,), (1,)), ((), ())), preferred_element_type=F32)
            if causal is not None:
                s = jnp.where(causal, s, NEG_INF)
            m_prev = m_sc[h]
            m_new = jnp.maximum(m_prev, jnp.max(s, -1, keepdims=True))
            alpha = jnp.exp2(m_prev - m_new)
            p = jnp.exp2(s - jnp.concatenate([m_new] * (tk // LANES), axis=1))
            v1 = jnp.concatenate([v, jnp.ones_like(v)], axis=1)
            acc_sc[h] = (jnp.concatenate([alpha, alpha], axis=1) * acc_sc[h]
                         + jnp.dot(p.astype(BF16), v1, preferred_element_type=F32))
            m_sc[h] = m_new

    def two_tiles(kk, carry):
        tile(pl.multiple_of(2 * kk * tk, tk), None)
        tile(pl.multiple_of((2 * kk + 1) * tk, tk), None)
        return carry

    lax.fori_loop(0, qi // 2, two_tiles, 0)

    @pl.when(qi % 2 == 1)
    def _():
        tile(pl.multiple_of((qi - 1) * tk, tk), None)

    shape = (q_ref.shape[0], tk)
    r = lax.broadcasted_iota(jnp.int32, shape, 0)
    c = lax.broadcasted_iota(jnp.int32, shape, 1)
    r_pos = (r & -TOKEN_GROUP) + _position_in_group(r)
    c_pos = (c & -TOKEN_GROUP) + _position_in_group(c)
    tile(pl.multiple_of(qi * tk, tk), c_pos <= r_pos)
    for h in range(MLA_HEADS):
        acc = acc_sc[h]
        o_ref[:, h * V_DIM:(h + 1) * V_DIM] = acc[:, :V_DIM] / acc[:, V_DIM:]


def _mla_attention(qm, km, vm, *, batch, seq, tq):
    t = qm.shape[0]
    nq = seq // tq
    q_map = lambda b, qi: (b * nq + qi, 0)
    kv_map = lambda b, qi: (b, 0)
    return pl.pallas_call(
        functools.partial(_mla_kernel, tk=tq),
        grid=(batch, nq),
        in_specs=[
            pl.BlockSpec((tq, MLA_HEADS * MLA_KPAD), q_map),
            pl.BlockSpec((seq, MLA_HEADS * MLA_KPAD), kv_map),
            pl.BlockSpec((seq, MLA_WIDTH), kv_map),
        ],
        out_specs=pl.BlockSpec((tq, MLA_WIDTH), q_map),
        out_shape=jax.ShapeDtypeStruct((t, MLA_WIDTH), F32),
        scratch_shapes=[pltpu.VMEM((MLA_HEADS, tq, LANES), F32),
                        pltpu.VMEM((MLA_HEADS, tq, 2 * V_DIM), F32)],
        compiler_params=_params(2),
        name="mla_attention",
    )(qm, km, vm)


HEAD_PAIRS = A_WIDTH // LANES


def _sub_index(i, dilation):
    if dilation == 16:
        return i
    if dilation == 4:
        return 32 * (i >> 5) + 4 * (i & 7) + ((i >> 3) & 3)
    return _position_in_group(i)


def _band_bias(dilation, has_prev):
    i = lax.broadcasted_iota(jnp.int32, (2 * SPAN, 2 * SPAN), 0) & (SPAN - 1)
    j = lax.broadcasted_iota(jnp.int32, (2 * SPAN, 2 * SPAN), 1)
    dist = SPAN + _sub_index(i, dilation) - ((j & SPAN) + _sub_index(j & (SPAN - 1), dilation))
    valid = (dist >= 0) & (dist <= SPAN)
    if has_prev is not None:
        valid = valid & ((j >= SPAN) | has_prev)
    return jnp.where(valid, 0.0, NEG_INF)


def _dilated_kernel(q_ref, k_ref, v_ref, o_ref, kp_sc, vp_sc, m_sc, l_sc):
    has_prev = pl.program_id(1) > 0
    lane = lax.broadcasted_iota(jnp.int32, (SPAN, LANES), 1)
    head_a = lane < A_HEAD_DIM
    ones = jnp.ones((2 * SPAN, LANES), BF16)

    def rows2d(t):
        return t.reshape(-1, LANES)

    def attend(q, k, v, bias, idx, first, last):
        qa = jnp.where(head_a, q, 0.0).astype(BF16)
        qb = jnp.where(head_a, 0.0, q).astype(BF16)
        q2 = jnp.concatenate([qa, qb], axis=0)
        s = lax.dot_general(q2, k.astype(BF16), (((1,), (1,)), ((), ())), preferred_element_type=F32) + bias
        m = jnp.max(s, -1, keepdims=True)
        p = jnp.exp2(s - m)
        pv = jnp.dot(p.astype(BF16), jnp.concatenate([v.astype(BF16), ones], axis=1),
                     preferred_element_type=F32)
        acc = jnp.where(head_a, pv[:SPAN, :LANES], pv[SPAN:, :LANES])
        l = jnp.where(head_a, pv[:SPAN, LANES:], pv[SPAN:, LANES:])
        m_cur = jnp.where(head_a, jnp.broadcast_to(m[:SPAN], (SPAN, LANES)),
                          jnp.broadcast_to(m[SPAN:], (SPAN, LANES)))
        shape = o_ref[idx].shape
        if first:
            o_ref[idx] = acc.reshape(shape)
            l_sc[idx] = l.reshape(shape)
            m_sc[idx] = m_cur.reshape(shape)
            return
        m_old = rows2d(m_sc[idx])
        m_new = jnp.maximum(m_old, m_cur)
        a_old = jnp.exp2(m_old - m_new)
        a_cur = jnp.exp2(m_cur - m_new)
        acc = a_old * rows2d(o_ref[idx]) + a_cur * acc
        l = a_old * rows2d(l_sc[idx]) + a_cur * l
        if last:
            o_ref[idx] = (acc / l).reshape(shape)
        else:
            o_ref[idx] = acc.reshape(shape)
            l_sc[idx] = l.reshape(shape)
            m_sc[idx] = m_new.reshape(shape)

    every = slice(None)

    band = _band_bias(1, None)
    band_first = _band_bias(1, has_prev)
    for p in range(HEAD_PAIRS):
        cols = pl.ds(p * LANES, LANES)
        k = jnp.concatenate([rows2d(kp_sc[CHUNK_GROUPS - 1, :, :, cols]), rows2d(k_ref[0, :, :, cols])], axis=0)
        v = jnp.concatenate([rows2d(vp_sc[CHUNK_GROUPS - 1, :, :, cols]), rows2d(v_ref[0, :, :, cols])], axis=0)
        attend(rows2d(q_ref[0, :, :, cols]), k, v, band_first, (0, every, every, cols), True, False)

    def d1_body(g, carry):
        for p in range(HEAD_PAIRS):
            cols = pl.ds(p * LANES, LANES)
            attend(rows2d(q_ref[g, :, :, cols]), rows2d(k_ref[pl.ds(g - 1, 2), :, :, cols]),
                   rows2d(v_ref[pl.ds(g - 1, 2), :, :, cols]), band, (g, every, every, cols), True, False)
        return carry
    lax.fori_loop(1, CHUNK_GROUPS, d1_body, 0, unroll=3)

    band = _band_bias(4, None)
    band_first = _band_bias(4, has_prev)

    def d4_first(r4, carry):
        res = pl.ds(4 * r4, 4)
        for p in range(HEAD_PAIRS):
            cols = pl.ds(p * LANES, LANES)
            k = jnp.concatenate([rows2d(kp_sc[CHUNK_GROUPS - 4:, res, :, cols]), rows2d(k_ref[0:4, res, :, cols])],
                                axis=0)
            v = jnp.concatenate([rows2d(vp_sc[CHUNK_GROUPS - 4:, res, :, cols]), rows2d(v_ref[0:4, res, :, cols])],
                                axis=0)
            attend(rows2d(q_ref[0:4, res, :, cols]), k, v, band_first, (slice(0, 4), res, every, cols), False, False)
        return carry
    lax.fori_loop(0, 4, d4_first, 0, unroll=4)

    blocks_after_first = CHUNK_GROUPS // 4 - 1

    def d4_body(t, carry):
        res = pl.ds(4 * (t // blocks_after_first), 4)
        g0 = 4 * (1 + t % blocks_after_first)
        for p in range(HEAD_PAIRS):
            cols = pl.ds(p * LANES, LANES)
            attend(rows2d(q_ref[pl.ds(g0, 4), res, :, cols]), rows2d(k_ref[pl.ds(g0 - 4, 8), res, :, cols]),
                   rows2d(v_ref[pl.ds(g0 - 4, 8), res, :, cols]), band, (pl.ds(g0, 4), res, every, cols),
                   False, False)
        return carry
    lax.fori_loop(0, 4 * blocks_after_first, d4_body, 0, unroll=4)

    band16 = _band_bias(16, has_prev)

    def d16_body(r, carry):
        for p in range(HEAD_PAIRS):
            cols = pl.ds(p * LANES, LANES)
            k = jnp.concatenate([rows2d(kp_sc[:, r, :, cols]), rows2d(k_ref[:, r, :, cols])], axis=0)
            v = jnp.concatenate([rows2d(vp_sc[:, r, :, cols]), rows2d(v_ref[:, r, :, cols])], axis=0)
            attend(rows2d(q_ref[:, r, :, cols]), k, v, band16, (every, r, every, cols), False, True)
        return carry
    lax.fori_loop(0, MAX_DILATION, d16_body, 0, unroll=4)

    kp_sc[...] = k_ref[...]
    vp_sc[...] = v_ref[...]


def _dilated_attention(qa, ka, va, *, batch, seq):
    t = qa.shape[0]
    chunks = seq // (CHUNK_GROUPS * TOKEN_GROUP)
    tiles = TOKEN_GROUP // SUBLANES
    shape4 = (t // TOKEN_GROUP, tiles, SUBLANES, A_WIDTH)
    view = lambda a: a.reshape(shape4)
    block = (CHUNK_GROUPS, tiles, SUBLANES, A_WIDTH)
    spec = pl.BlockSpec(block, lambda b, c: (b * chunks + c, 0, 0, 0))
    scratch = pltpu.VMEM(block, F32)
    out = pl.pallas_call(
        _dilated_kernel_entry,
        grid=(batch, chunks),
        in_specs=[spec, spec, spec],
        out_specs=spec,
        out_shape=jax.ShapeDtypeStruct(shape4, F32),
        scratch_shapes=[scratch, scratch, scratch, scratch],
        compiler_params=_params(2),
        name="dilated",
    )(view(qa), view(ka), view(va))
    return out.reshape(t, A_WIDTH)


def _dilated_kernel_entry(q_ref, k_ref, v_ref, o_ref, kp_sc, vp_sc, m_sc, l_sc):
    @pl.when((pl.program_id(0) == 0) & (pl.program_id(1) == 0))
    def _():
        kp_sc[...] = jnp.zeros(kp_sc.shape, F32)
        vp_sc[...] = jnp.zeros(vp_sc.shape, F32)

    _dilated_kernel(q_ref, k_ref, v_ref, o_ref, kp_sc, vp_sc, m_sc, l_sc)


FF_CHUNK = 1024


def _mix_ffn_kernel(a_ref, b_ref, x_ref, wo_ref, ga_ref, gb_ref, g1_ref, beta1_ref,
                    w1_ref, w2_ref, g2_ref, beta2_ref, o_ref):
    mixed = jnp.concatenate([_rms(a_ref[...], ga_ref[...]), _rms(b_ref[...], gb_ref[...])], axis=1)
    y = jnp.dot(mixed.astype(BF16), wo_ref[...], preferred_element_type=F32)
    x1 = _layer_norm(ALPHA * x_ref[...] + y, g1_ref[...], beta1_ref[...])
    xb = x1.astype(BF16)
    acc = ALPHA * x1
    for c in range(D_FF // FF_CHUNK):
        cols = slice(c * FF_CHUNK, (c + 1) * FF_CHUNK)
        hdn = jnp.dot(xb, w1_ref[:, cols], preferred_element_type=F32)
        hdn = jnp.square(jnp.maximum(hdn, 0.0)).astype(BF16)
        acc = acc + jnp.dot(hdn, w2_ref[cols, :], preferred_element_type=F32)
    o_ref[...] = _layer_norm(acc, g2_ref[...], beta2_ref[...])


def _mix_ffn(a, b, x, wo, ga, gb, g1, beta1, w1, w2, g2, beta2, *, tm):
    t = x.shape[0]
    row = lambda i: (i, 0)
    consts = (wo, ga, gb, g1, beta1, w1, w2, g2, beta2)
    return pl.pallas_call(
        _mix_ffn_kernel,
        grid=(t // tm,),
        in_specs=[pl.BlockSpec((tm, A_WIDTH), row), pl.BlockSpec((tm, MLA_WIDTH), row),
                  pl.BlockSpec((tm, D_MODEL), row)] + [_const_spec(c.shape) for c in consts],
        out_specs=pl.BlockSpec((tm, D_MODEL), row),
        out_shape=jax.ShapeDtypeStruct((t, D_MODEL), F32),
        compiler_params=_params(1),
        name="mix_ffn",
    )(a, b, x, *consts)


def _permute_kernel(x_ref, o_ref, slab_sc, *, inverse):
    tm, width = x_ref.shape
    tiles = TOKEN_GROUP // SUBLANES
    for g in range(tm // TOKEN_GROUP):
        for tile in range(tiles):
            first_token = 4 * (tile & 3) + (tile >> 2)
            natural = pl.ds(g * TOKEN_GROUP + first_token, SUBLANES, stride=tiles)
            permuted = pl.ds(g * TOKEN_GROUP + tile * SUBLANES, SUBLANES)
            for c in range(width // LANES):
                cols = pl.ds(c * LANES, LANES)
                if inverse:
                    slab_sc[c, natural, :] = x_ref[permuted, cols]
                else:
                    if g == 0 and tile == 0:
                        slab_sc[c] = x_ref[:, cols]
                    o_ref[permuted, cols] = slab_sc[c, natural, :]
    if inverse:
        for c in range(width // LANES):
            o_ref[:, pl.ds(c * LANES, LANES)] = slab_sc[c]


def _permute_tokens(x, inverse=False, *, tm=512):
    t, d = x.shape
    row = lambda i: (i, 0)
    return pl.pallas_call(
        functools.partial(_permute_kernel, inverse=inverse),
        grid=(t // tm,),
        in_specs=[pl.BlockSpec((tm, d), row)],
        out_specs=pl.BlockSpec((tm, d), row),
        out_shape=jax.ShapeDtypeStruct((t, d), x.dtype),
        scratch_shapes=[pltpu.VMEM((d // LANES, tm, LANES), x.dtype)],
        compiler_params=_params(1),
        name="unpermute" if inverse else "permute",
    )(x)


def _rope_tables(seq):
    half = A_HEAD_DIM // 2
    inv_freq = ROPE_THETA ** (-jnp.arange(half, dtype=F32) / half)
    row = jnp.arange(seq, dtype=jnp.int32)
    pos = (row & -TOKEN_GROUP) + _position_in_group(row)
    ang = pos.astype(F32)[:, None] * inv_freq[None, :]
    cos = jnp.cos(ang)
    sin = jnp.sin(ang)
    cos128 = jnp.tile(cos, (1, LANES // half))
    sin128 = jnp.tile(jnp.concatenate([-sin, sin], axis=1), (1, LANES // A_HEAD_DIM))
    return cos128, sin128


def _prep_w_uq(w):
    w = w.reshape(Q_LORA, MLA_HEADS, QK_NOPE + QK_ROPE)
    w = jnp.pad(w, ((0, 0), (0, 0), (0, MLA_KPAD - QK_NOPE - QK_ROPE)))
    return w.reshape(Q_LORA, MLA_HEADS * MLA_KPAD).astype(BF16)


def _prep_w_ukv(w):
    w = w.reshape(KV_LORA, MLA_HEADS, QK_NOPE + V_DIM)
    k = w[:, :, :QK_NOPE].reshape(KV_LORA, MLA_HEADS * QK_NOPE)
    v = w[:, :, QK_NOPE:].reshape(KV_LORA, MLA_HEADS * V_DIM)
    return jnp.concatenate([k, v], axis=1).astype(BF16)


def kernel(x, w_in, q_a_norm, kv_a_norm, w_uq, w_ukv, a_out_norm, b_out_norm, w_o,
           ln1_g, ln1_b, w_ff1, w_ff2, ln2_g, ln2_b):
    batch, seq, _ = x.shape
    assert seq % (CHUNK_GROUPS * TOKEN_GROUP) == 0
    t = batch * seq
    cos, sin = _rope_tables(seq)
    xf = _permute_tokens(x.reshape(t, D_MODEL))
    row2d = lambda v: v.reshape(1, -1)
    for l in range(DEPTH):
        w_in_l = jnp.pad(w_in[l], ((0, 0), (0, IN_COLS_PAD - w_in.shape[2]))).astype(BF16)
        qa, ka, va, qm, km, vm = _in_proj(
            xf, w_in_l, _prep_w_uq(w_uq[l]), _prep_w_ukv(w_ukv[l]),
            row2d(q_a_norm[l]), row2d(kv_a_norm[l]), cos, sin, seq=seq, tm=512)
        b_out = _mla_attention(qm, km, vm, batch=batch, seq=seq, tq=512)
        a_out = _dilated_attention(qa, ka, va, batch=batch, seq=seq)
        xf = _mix_ffn(a_out, b_out, xf, w_o[l].astype(BF16), row2d(a_out_norm[l]), row2d(b_out_norm[l]),
                      row2d(ln1_g[l]), row2d(ln1_b[l]), w_ff1[l].astype(BF16), w_ff2[l].astype(BF16),
                      row2d(ln2_g[l]), row2d(ln2_b[l]), tm=512)
    return _permute_tokens(xf, inverse=True).reshape(batch, seq, D_MODEL)
```

```python
import functools

import jax
import jax.numpy as jnp
from jax import lax
from jax.experimental import pallas as pl
from jax.experimental.pallas import tpu as pltpu

D_MODEL = 1024
DEPTH = 4
A_HEADS = 8
A_HEAD_DIM = 64
A_WIDTH = A_HEADS * A_HEAD_DIM
MLA_HEADS = 4
QK_NOPE = 128
QK_ROPE = 64
V_DIM = 128
Q_LORA = 256
KV_LORA = 128
MLA_WIDTH = MLA_HEADS * V_DIM
D_FF = 4 * D_MODEL
ROPE_THETA = 10000.0
ALPHA = (2.0 * DEPTH) ** 0.25
LN_EPS = 1e-5
RMS_EPS = 1e-6

LANES = 128
SUBLANES = 8
MLA_KPAD = 256
IN_COLS_PAD = 3 * A_WIDTH + Q_LORA + KV_LORA + LANES
VMEM_LIMIT = 56 * 1024 * 1024
STAGE_VMEM_LIMIT = 60 * 1024 * 1024
STAGE_TM = 512

SPAN = 128
MAX_DILATION = 16
TOKEN_GROUP = SPAN
CHUNK_GROUPS = SPAN * MAX_DILATION // TOKEN_GROUP

F32 = jnp.float32
BF16 = jnp.bfloat16
NEG_INF = float("-inf")
LOG2E = 1.4426950408889634
MLA_Q_SCALE = (QK_NOPE + QK_ROPE) ** -0.5 * LOG2E
A_Q_SCALE = A_HEAD_DIM ** -0.5 * LOG2E


def _const_spec(shape):
    return pl.BlockSpec(shape, lambda *_: (0,) * len(shape), pipeline_mode=pl.Buffered(1))


def _params(n_axes, vmem_limit=VMEM_LIMIT):
    return pltpu.CompilerParams(dimension_semantics=("arbitrary",) * n_axes,
                                vmem_limit_bytes=vmem_limit)


def _rms(x, g):
    return x * lax.rsqrt(jnp.mean(x * x, -1, keepdims=True) + RMS_EPS) * g


def _layer_norm(z, g, b):
    mu = jnp.mean(z, -1, keepdims=True)
    zc = z - mu
    var = jnp.mean(zc * zc, -1, keepdims=True)
    return zc * lax.rsqrt(var + LN_EPS) * g + b


def _rope128(x, cos, sin_signed, first_half):
    rot = jnp.where(first_half, pltpu.roll(x, 96, 1), pltpu.roll(x, 32, 1))
    return x * cos + rot * sin_signed


def _position_in_group(i):
    return 16 * (i & 7) + 4 * ((i >> 3) & 3) + ((i >> 5) & 3)


FF_CHUNK = 1024
IN_PROJ_OUTS = ((A_WIDTH, F32), (A_WIDTH, F32), (A_WIDTH, F32),
                (MLA_HEADS * MLA_KPAD, BF16), (MLA_HEADS * MLA_KPAD, BF16), (MLA_WIDTH, BF16))


def _in_proj_tail(x, w_in_ref, w_uq_ref, w_ukv_ref, gq_ref, gkv_ref, cos_ref, sin_ref,
                  qa_ref, ka_ref, va_ref, qm_ref, km_ref, vm_ref):
    h = jnp.dot(x.astype(BF16), w_in_ref[...], preferred_element_type=F32)
    cos = cos_ref[...]
    sin = sin_ref[...]
    lane = lax.broadcasted_iota(jnp.int32, cos.shape, 1)
    first_half = (lane & (A_HEAD_DIM - 1)) < (A_HEAD_DIM // 2)

    def rope(t):
        return _rope128(t, cos, sin, first_half)

    def rope_cols(t):
        return jnp.concatenate(
            [rope(t[:, c * LANES:(c + 1) * LANES]) for c in range(t.shape[1] // LANES)], axis=1)

    qa_ref[...] = rope_cols(h[:, 0:A_WIDTH]) * A_Q_SCALE
    ka_ref[...] = rope_cols(h[:, A_WIDTH:2 * A_WIDTH])
    va_ref[...] = h[:, 2 * A_WIDTH:3 * A_WIDTH]

    c0 = 3 * A_WIDTH
    cq = _rms(h[:, c0:c0 + Q_LORA], gq_ref[...])
    q = jnp.dot(cq.astype(BF16), w_uq_ref[...], preferred_element_type=F32)
    q_parts = []
    for hh in range(MLA_HEADS):
        base = hh * MLA_KPAD
        q_parts.append(q[:, base:base + QK_NOPE])
        q_parts.append(rope(q[:, base + QK_NOPE:base + MLA_KPAD]))
    qm_ref[...] = (jnp.concatenate(q_parts, axis=1) * MLA_Q_SCALE).astype(BF16)

    c1 = c0 + Q_LORA
    ckv = _rms(h[:, c1:c1 + KV_LORA], gkv_ref[...])
    kv = jnp.dot(ckv.astype(BF16), w_ukv_ref[...], preferred_element_type=F32)
    c2 = c1 + KV_LORA
    k_pe = rope(h[:, c2:c2 + LANES])
    k_parts = []
    for hh in range(MLA_HEADS):
        k_parts.append(kv[:, hh * QK_NOPE:(hh + 1) * QK_NOPE])
        k_parts.append(k_pe)
    km_ref[...] = jnp.concatenate(k_parts, axis=1).astype(BF16)
    vm_ref[...] = kv[:, MLA_HEADS * QK_NOPE:].astype(BF16)


def _mix_ffn_head(a_ref, b_ref, x_ref, wo_ref, ga_ref, gb_ref, g1_ref, beta1_ref,
                  w1_ref, w2_ref, g2_ref, beta2_ref):
    mixed = jnp.concatenate([_rms(a_ref[...], ga_ref[...]), _rms(b_ref[...], gb_ref[...])], axis=1)
    y = jnp.dot(mixed.astype(BF16), wo_ref[...], preferred_element_type=F32)
    x1 = _layer_norm(ALPHA * x_ref[...] + y, g1_ref[...], beta1_ref[...])
    xb = x1.astype(BF16)
    acc = ALPHA * x1
    for c in range(D_FF // FF_CHUNK):
        cols = slice(c * FF_CHUNK, (c + 1) * FF_CHUNK)
        hdn = jnp.dot(xb, w1_ref[:, cols], preferred_element_type=F32)
        hdn = jnp.square(jnp.maximum(hdn, 0.0)).astype(BF16)
        acc = acc + jnp.dot(hdn, w2_ref[cols, :], preferred_element_type=F32)
    return _layer_norm(acc, g2_ref[...], beta2_ref[...])


def _permute_rows(src, o_ref, slab_sc, *, inverse):
    tm, width = o_ref.shape
    tiles = TOKEN_GROUP // SUBLANES
    if not inverse:
        for c in range(width // LANES):
            slab_sc[c] = src[:, c * LANES:(c + 1) * LANES]
    for g in range(tm // TOKEN_GROUP):
        for tile in range(tiles):
            first_token = 4 * (tile & 3) + (tile >> 2)
            natural = pl.ds(g * TOKEN_GROUP + first_token, SUBLANES, stride=tiles)
            p0 = g * TOKEN_GROUP + tile * SUBLANES
            for c in range(width // LANES):
                if inverse:
                    slab_sc[c, natural, :] = src[p0:p0 + SUBLANES, c * LANES:(c + 1) * LANES]
                else:
                    o_ref[p0:p0 + SUBLANES, c * LANES:(c + 1) * LANES] = slab_sc[c, natural, :]
    if inverse:
        for c in range(width // LANES):
            o_ref[:, c * LANES:(c + 1) * LANES] = slab_sc[c]


N_MIX_INPUTS = 12
N_IN_PROJ_INPUTS = 7


def _stage_kernel(*refs, first, last):
    refs = list(refs)
    if first:
        head_refs, refs = refs[:1], refs[1:]
    else:
        head_refs, refs = refs[:N_MIX_INPUTS], refs[N_MIX_INPUTS:]
    if not last:
        tail_in, refs = refs[:N_IN_PROJ_INPUTS], refs[N_IN_PROJ_INPUTS:]
    x_out_ref, refs = refs[0], refs[1:]
    if first:
        _permute_rows(head_refs[0], x_out_ref, refs[-1], inverse=False)
        x_new = x_out_ref[...]
    else:
        x_new = _mix_ffn_head(*head_refs)
        if last:
            _permute_rows(x_new, x_out_ref, refs[-1], inverse=True)
        else:
            x_out_ref[...] = x_new
    if not last:
        _in_proj_tail(x_new, *tail_in, *refs[:len(IN_PROJ_OUTS)])


def _stage(head_args, tail_args, *, first, last, seq, tm=STAGE_TM):
    x = head_args[0] if first else head_args[2]
    t = x.shape[0]
    row = lambda i: (i, 0)
    n_pos_blocks = seq // tm
    pos = lambda i: (i % n_pos_blocks, 0)
    if first:
        in_specs = [pl.BlockSpec((tm, D_MODEL), row)]
    else:
        in_specs = [pl.BlockSpec((tm, A_WIDTH), row), pl.BlockSpec((tm, MLA_WIDTH), row),
                    pl.BlockSpec((tm, D_MODEL), row)] + [_const_spec(c.shape) for c in head_args[3:]]
    out_specs = [pl.BlockSpec((tm, D_MODEL), row)]
    out_shape = [jax.ShapeDtypeStruct((t, D_MODEL), F32)]
    if not last:
        in_specs += [_const_spec(c.shape) for c in tail_args[:5]]
        in_specs += [pl.BlockSpec((tm, LANES), pos), pl.BlockSpec((tm, LANES), pos)]
        out_specs += [pl.BlockSpec((tm, w), row) for w, _ in IN_PROJ_OUTS]
        out_shape += [jax.ShapeDtypeStruct((t, w), dt) for w, dt in IN_PROJ_OUTS]
    scratch = [pltpu.VMEM((D_MODEL // LANES, tm, LANES), F32)] if (first or last) else []
    return pl.pallas_call(
        functools.partial(_stage_kernel, first=first, last=last),
        grid=(t // tm,),
        in_specs=in_specs, out_specs=out_specs, out_shape=out_shape, scratch_shapes=scratch,
        compiler_params=_params(1, STAGE_VMEM_LIMIT),
        name="stage_first" if first else ("stage_last" if last else "stage"),
    )(*head_args, *tail_args)


Q_TILES = 2


def _mla_kernel(q_ref, k_ref, v_ref, o_ref, m_sc, acc_sc, *, tq):
    first_diag = Q_TILES * pl.program_id(1)
    tk = tq
    m_sc[...] = jnp.full(m_sc.shape, NEG_INF, F32)
    acc_sc[...] = jnp.zeros(acc_sc.shape, F32)

    def tile(qt, k_tile, causal):
        k_start = pl.multiple_of(k_tile * tk, tk)
        for h in range(MLA_HEADS):
            q = q_ref[qt * tq:(qt + 1) * tq, h * MLA_KPAD:(h + 1) * MLA_KPAD]
            k = k_ref[pl.ds(k_start, tk), h * MLA_KPAD:(h + 1) * MLA_KPAD]
            v = v_ref[pl.ds(k_start, tk), h * V_DIM:(h + 1) * V_DIM]
            s = lax.dot_general(q, k, (((1,), (1,)), ((), ())), preferred_element_type=F32)
            if causal is not None:
                s = jnp.where(causal, s, NEG_INF)
            m_prev = m_sc[qt, h]
            m_new = jnp.maximum(m_prev, jnp.max(s, -1, keepdims=True))
            alpha = jnp.exp2(m_prev - m_new)
            p = jnp.exp2(s - jnp.concatenate([m_new] * (tk // LANES), axis=1))
            v1 = jnp.concatenate([v, jnp.ones_like(v)], axis=1)
            acc_sc[qt, h] = (jnp.concatenate([alpha, alpha], axis=1) * acc_sc[qt, h]
                             + jnp.dot(p.astype(BF16), v1, preferred_element_type=F32))
            m_sc[qt, h] = m_new

    def below_diagonal(k_tile, carry):
        for qt in range(Q_TILES):
            tile(qt, k_tile, None)
        return carry

    lax.fori_loop(0, first_diag, below_diagonal, 0)

    r = lax.broadcasted_iota(jnp.int32, (tq, tk), 0)
    c = lax.broadcasted_iota(jnp.int32, (tq, tk), 1)
    causal = ((c & -TOKEN_GROUP) + _position_in_group(c)) <= ((r & -TOKEN_GROUP) + _position_in_group(r))
    for qt in range(Q_TILES):
        for kt in range(qt + 1):
            tile(qt, first_diag + kt, causal if kt == qt else None)
    for qt in range(Q_TILES):
        for h in range(MLA_HEADS):
            acc = acc_sc[qt, h]
            o_ref[qt * tq:(qt + 1) * tq, h * V_DIM:(h + 1) * V_DIM] = acc[:, :V_DIM] / acc[:, V_DIM:]


def _mla_attention(qm, km, vm, *, batch, seq, tq):
    t = qm.shape[0]
    steps = seq // (Q_TILES * tq)
    q_map = lambda b, i: (b * steps + i, 0)
    kv_map = lambda b, i: (b, 0)
    return pl.pallas_call(
        functools.partial(_mla_kernel, tq=tq),
        grid=(batch, steps),
        in_specs=[
            pl.BlockSpec((Q_TILES * tq, MLA_HEADS * MLA_KPAD), q_map),
            pl.BlockSpec((seq, MLA_HEADS * MLA_KPAD), kv_map),
            pl.BlockSpec((seq, MLA_WIDTH), kv_map),
        ],
        out_specs=pl.BlockSpec((Q_TILES * tq, MLA_WIDTH), q_map),
        out_shape=jax.ShapeDtypeStruct((t, MLA_WIDTH), F32),
        scratch_shapes=[pltpu.VMEM((Q_TILES, MLA_HEADS, tq, LANES), F32),
                        pltpu.VMEM((Q_TILES, MLA_HEADS, tq, 2 * V_DIM), F32)],
        compiler_params=_params(2),
        name="mla_attention",
    )(qm, km, vm)


HEAD_PAIRS = A_WIDTH // LANES


def _sub_index(i, dilation):
    if dilation == 16:
        return i
    if dilation == 4:
        return 32 * (i >> 5) + 4 * (i & 7) + ((i >> 3) & 3)
    return _position_in_group(i)


def _band_bias(dilation, has_prev):
    i = lax.broadcasted_iota(jnp.int32, (2 * SPAN, 2 * SPAN), 0) & (SPAN - 1)
    j = lax.broadcasted_iota(jnp.int32, (2 * SPAN, 2 * SPAN), 1)
    dist = SPAN + _sub_index(i, dilation) - ((j & SPAN) + _sub_index(j & (SPAN - 1), dilation))
    valid = (dist >= 0) & (dist <= SPAN)
    if has_prev is not None:
        valid = valid & ((j >= SPAN) | has_prev)
    return jnp.where(valid, 0.0, NEG_INF)


def _dilated_kernel(q_ref, k_ref, v_ref, o_ref, kp_sc, vp_sc, m_sc, l_sc):
    @pl.when((pl.program_id(0) == 0) & (pl.program_id(1) == 0))
    def _():
        kp_sc[...] = jnp.zeros(kp_sc.shape, F32)
        vp_sc[...] = jnp.zeros(vp_sc.shape, F32)

    has_prev = pl.program_id(1) > 0
    lane = lax.broadcasted_iota(jnp.int32, (SPAN, LANES), 1)
    head_a = lane < A_HEAD_DIM
    ones = jnp.ones((2 * SPAN, LANES), BF16)

    def rows2d(t):
        return t.reshape(-1, LANES)

    def attend(q, k, v, bias, idx, first, last):
        qa = jnp.where(head_a, q, 0.0).astype(BF16)
        qb = jnp.where(head_a, 0.0, q).astype(BF16)
        q2 = jnp.concatenate([qa, qb], axis=0)
        s = lax.dot_general(q2, k.astype(BF16), (((1,), (1,)), ((), ())), preferred_element_type=F32) + bias
        m = jnp.max(s, -1, keepdims=True)
        p = jnp.exp2(s - m)
        pv = jnp.dot(p.astype(BF16), jnp.concatenate([v.astype(BF16), ones], axis=1),
                     preferred_element_type=F32)
        acc = jnp.where(head_a, pv[:SPAN, :LANES], pv[SPAN:, :LANES])
        l = jnp.where(head_a, pv[:SPAN, LANES:], pv[SPAN:, LANES:])
        m_cur = jnp.where(head_a, jnp.broadcast_to(m[:SPAN], (SPAN, LANES)),
                          jnp.broadcast_to(m[SPAN:], (SPAN, LANES)))
        shape = o_ref[idx].shape
        if first:
            o_ref[idx] = acc.reshape(shape)
            l_sc[idx] = l.reshape(shape)
            m_sc[idx] = m_cur.reshape(shape)
            return
        m_old = rows2d(m_sc[idx])
        m_new = jnp.maximum(m_old, m_cur)
        a_old = jnp.exp2(m_old - m_new)
        a_cur = jnp.exp2(m_cur - m_new)
        acc = a_old * rows2d(o_ref[idx]) + a_cur * acc
        l = a_old * rows2d(l_sc[idx]) + a_cur * l
        if last:
            o_ref[idx] = (acc / l).reshape(shape)
        else:
            o_ref[idx] = acc.reshape(shape)
            l_sc[idx] = l.reshape(shape)
            m_sc[idx] = m_new.reshape(shape)

    every = slice(None)

    band = _band_bias(1, None)
    band_first = _band_bias(1, has_prev)
    for p in range(HEAD_PAIRS):
        cols = pl.ds(p * LANES, LANES)
        k = jnp.concatenate([rows2d(kp_sc[CHUNK_GROUPS - 1, :, :, cols]), rows2d(k_ref[0, :, :, cols])], axis=0)
        v = jnp.concatenate([rows2d(vp_sc[CHUNK_GROUPS - 1, :, :, cols]), rows2d(v_ref[0, :, :, cols])], axis=0)
        attend(rows2d(q_ref[0, :, :, cols]), k, v, band_first, (0, every, every, cols), True, False)

    def d1_body(g, carry):
        for p in range(HEAD_PAIRS):
            cols = pl.ds(p * LANES, LANES)
            attend(rows2d(q_ref[g, :, :, cols]), rows2d(k_ref[pl.ds(g - 1, 2), :, :, cols]),
                   rows2d(v_ref[pl.ds(g - 1, 2), :, :, cols]), band, (g, every, every, cols), True, False)
        return carry
    lax.fori_loop(1, CHUNK_GROUPS, d1_body, 0, unroll=3)

    band = _band_bias(4, None)
    band_first = _band_bias(4, has_prev)

    def d4_first(r4, carry):
        res = pl.ds(4 * r4, 4)
        for p in range(HEAD_PAIRS):
            cols = pl.ds(p * LANES, LANES)
            k = jnp.concatenate([rows2d(kp_sc[CHUNK_GROUPS - 4:, res, :, cols]), rows2d(k_ref[0:4, res, :, cols])],
                                axis=0)
            v = jnp.concatenate([rows2d(vp_sc[CHUNK_GROUPS - 4:, res, :, cols]), rows2d(v_ref[0:4, res, :, cols])],
                                axis=0)
            attend(rows2d(q_ref[0:4, res, :, cols]), k, v, band_first, (slice(0, 4), res, every, cols), False, False)
        return carry
    lax.fori_loop(0, 4, d4_first, 0, unroll=4)

    blocks_after_first = CHUNK_GROUPS // 4 - 1

    def d4_body(t, carry):
        res = pl.ds(4 * (t // blocks_after_first), 4)
        g0 = 4 * (1 + t % blocks_after_first)
        for p in range(HEAD_PAIRS):
            cols = pl.ds(p * LANES, LANES)
            attend(rows2d(q_ref[pl.ds(g0, 4), res, :, cols]), rows2d(k_ref[pl.ds(g0 - 4, 8), res, :, cols]),
                   rows2d(v_ref[pl.ds(g0 - 4, 8), res, :, cols]), band, (pl.ds(g0, 4), res, every, cols),
                   False, False)
        return carry
    lax.fori_loop(0, 4 * blocks_after_first, d4_body, 0, unroll=4)

    band16 = _band_bias(16, has_prev)

    def d16_body(r, carry):
        for p in range(HEAD_PAIRS):
            cols = pl.ds(p * LANES, LANES)
            k = jnp.concatenate([rows2d(kp_sc[:, r, :, cols]), rows2d(k_ref[:, r, :, cols])], axis=0)
            v = jnp.concatenate([rows2d(vp_sc[:, r, :, cols]), rows2d(v_ref[:, r, :, cols])], axis=0)
            attend(rows2d(q_ref[:, r, :, cols]), k, v, band16, (every, r, every, cols), False, True)
        return carry
    lax.fori_loop(0, MAX_DILATION, d16_body, 0, unroll=4)

    kp_sc[...] = k_ref[...]
    vp_sc[...] = v_ref[...]


def _dilated_attention(qa, ka, va, *, batch, seq):
    t = qa.shape[0]
    chunks = seq // (CHUNK_GROUPS * TOKEN_GROUP)
    tiles = TOKEN_GROUP // SUBLANES
    shape4 = (t // TOKEN_GROUP, tiles, SUBLANES, A_WIDTH)
    view = lambda a: a.reshape(shape4)
    block = (CHUNK_GROUPS, tiles, SUBLANES, A_WIDTH)
    spec = pl.BlockSpec(block, lambda b, c: (b * chunks + c, 0, 0, 0))
    scratch = pltpu.VMEM(block, F32)
    out = pl.pallas_call(
        _dilated_kernel,
        grid=(batch, chunks),
        in_specs=[spec, spec, spec],
        out_specs=spec,
        out_shape=jax.ShapeDtypeStruct(shape4, F32),
        scratch_shapes=[scratch, scratch, scratch, scratch],
        compiler_params=_params(2),
        name="dilated",
    )(view(qa), view(ka), view(va))
    return out.reshape(t, A_WIDTH)


def _rope_tables(seq):
    half = A_HEAD_DIM // 2
    inv_freq = ROPE_THETA ** (-jnp.arange(half, dtype=F32) / half)
    row = jnp.arange(seq, dtype=jnp.int32)
    pos = (row & -TOKEN_GROUP) + _position_in_group(row)
    ang = pos.astype(F32)[:, None] * inv_freq[None, :]
    cos = jnp.cos(ang)
    sin = jnp.sin(ang)
    cos128 = jnp.tile(cos, (1, LANES // half))
    sin128 = jnp.tile(jnp.concatenate([-sin, sin], axis=1), (1, LANES // A_HEAD_DIM))
    return cos128, sin128


def _prep_w_uq(w):
    w = w.reshape(Q_LORA, MLA_HEADS, QK_NOPE + QK_ROPE)
    w = jnp.pad(w, ((0, 0), (0, 0), (0, MLA_KPAD - QK_NOPE - QK_ROPE)))
    return w.reshape(Q_LORA, MLA_HEADS * MLA_KPAD).astype(BF16)


def _prep_w_ukv(w):
    w = w.reshape(KV_LORA, MLA_HEADS, QK_NOPE + V_DIM)
    k = w[:, :, :QK_NOPE].reshape(KV_LORA, MLA_HEADS * QK_NOPE)
    v = w[:, :, QK_NOPE:].reshape(KV_LORA, MLA_HEADS * V_DIM)
    return jnp.concatenate([k, v], axis=1).astype(BF16)


def kernel(x, w_in, q_a_norm, kv_a_norm, w_uq, w_ukv, a_out_norm, b_out_norm, w_o,
           ln1_g, ln1_b, w_ff1, w_ff2, ln2_g, ln2_b):
    batch, seq, _ = x.shape
    assert seq % (CHUNK_GROUPS * TOKEN_GROUP) == 0
    t = batch * seq
    cos, sin = _rope_tables(seq)
    row2d = lambda v: v.reshape(1, -1)

    def in_proj_args(l):
        w_in_l = jnp.pad(w_in[l], ((0, 0), (0, IN_COLS_PAD - w_in.shape[2]))).astype(BF16)
        return (w_in_l, _prep_w_uq(w_uq[l]), _prep_w_ukv(w_ukv[l]), row2d(q_a_norm[l]), row2d(kv_a_norm[l]),
                cos, sin)

    def mix_args(l):
        return (w_o[l].astype(BF16), row2d(a_out_norm[l]), row2d(b_out_norm[l]), row2d(ln1_g[l]),
                row2d(ln1_b[l]), w_ff1[l].astype(BF16), w_ff2[l].astype(BF16), row2d(ln2_g[l]), row2d(ln2_b[l]))

    xf, qa, ka, va, qm, km, vm = _stage((x.reshape(t, D_MODEL),), in_proj_args(0),
                                        first=True, last=False, seq=seq)
    for l in range(DEPTH):
        b_out = _mla_attention(qm, km, vm, batch=batch, seq=seq, tq=512)
        a_out = _dilated_attention(qa, ka, va, batch=batch, seq=seq)
        head = (a_out, b_out, xf) + mix_args(l)
        if l + 1 < DEPTH:
            xf, qa, ka, va, qm, km, vm = _stage(head, in_proj_args(l + 1), first=False, last=False, seq=seq)
        else:
            (xf,) = _stage(head, (), first=False, last=True, seq=seq)
    return xf.reshape(batch, seq, D_MODEL)
```

```python
import functools

import jax
import jax.numpy as jnp
from jax import lax
from jax.experimental import pallas as pl
from jax.experimental.pallas import tpu as pltpu

D_MODEL = 1024
DEPTH = 4
A_HEADS = 8
A_HEAD_DIM = 64
A_WIDTH = A_HEADS * A_HEAD_DIM
MLA_HEADS = 4
QK_NOPE = 128
QK_ROPE = 64
V_DIM = 128
Q_LORA = 256
KV_LORA = 128
MLA_WIDTH = MLA_HEADS * V_DIM
D_FF = 4 * D_MODEL
ROPE_THETA = 10000.0
ALPHA = (2.0 * DEPTH) ** 0.25
LN_EPS = 1e-5
RMS_EPS = 1e-6

LANES = 128
SUBLANES = 8
MLA_KPAD = 256
IN_COLS_PAD = 3 * A_WIDTH + Q_LORA + KV_LORA + LANES
VMEM_LIMIT = 56 * 1024 * 1024
STAGE_VMEM_LIMIT = 60 * 1024 * 1024
STAGE_TM = 512

SPAN = 128
MAX_DILATION = 16
TOKEN_GROUP = SPAN
CHUNK_GROUPS = SPAN * MAX_DILATION // TOKEN_GROUP

F32 = jnp.float32
BF16 = jnp.bfloat16
NEG_INF = float("-inf")
LOG2E = 1.4426950408889634
MLA_Q_SCALE = (QK_NOPE + QK_ROPE) ** -0.5 * LOG2E
A_Q_SCALE = A_HEAD_DIM ** -0.5 * LOG2E


def _const_spec(shape):
    return pl.BlockSpec(shape, lambda *_: (0,) * len(shape), pipeline_mode=pl.Buffered(1))


def _params(n_axes, vmem_limit=VMEM_LIMIT):
    return pltpu.CompilerParams(dimension_semantics=("arbitrary",) * n_axes,
                                vmem_limit_bytes=vmem_limit)


def _rms(x, g):
    return x * lax.rsqrt(jnp.mean(x * x, -1, keepdims=True) + RMS_EPS) * g


def _layer_norm(z, g, b):
    mu = jnp.mean(z, -1, keepdims=True)
    zc = z - mu
    var = jnp.mean(zc * zc, -1, keepdims=True)
    return zc * lax.rsqrt(var + LN_EPS) * g + b


def _rope128(x, cos, sin_signed, first_half):
    rot = jnp.where(first_half, pltpu.roll(x, 96, 1), pltpu.roll(x, 32, 1))
    return x * cos + rot * sin_signed


def _position_in_group(i):
    return 16 * (i & 7) + 4 * ((i >> 3) & 3) + ((i >> 5) & 3)


FF_CHUNK = 1024
IN_PROJ_OUTS = ((A_WIDTH, F32), (A_WIDTH, F32), (A_WIDTH, F32),
                (MLA_HEADS * MLA_KPAD, BF16), (MLA_HEADS * MLA_KPAD, BF16), (MLA_WIDTH, BF16))


def _in_proj_tail(x, w_in_ref, w_uq_ref, w_ukv_ref, gq_ref, gkv_ref, cos_ref, sin_ref,
                  qa_ref, ka_ref, va_ref, qm_ref, km_ref, vm_ref):
    h = jnp.dot(x.astype(BF16), w_in_ref[...], preferred_element_type=F32)
    cos = cos_ref[...]
    sin = sin_ref[...]
    lane = lax.broadcasted_iota(jnp.int32, cos.shape, 1)
    first_half = (lane & (A_HEAD_DIM - 1)) < (A_HEAD_DIM // 2)

    def rope(t):
        return _rope128(t, cos, sin, first_half)

    def rope_cols(t):
        return jnp.concatenate(
            [rope(t[:, c * LANES:(c + 1) * LANES]) for c in range(t.shape[1] // LANES)], axis=1)

    qa_ref[...] = rope_cols(h[:, 0:A_WIDTH]) * A_Q_SCALE
    ka_ref[...] = rope_cols(h[:, A_WIDTH:2 * A_WIDTH])
    va_ref[...] = h[:, 2 * A_WIDTH:3 * A_WIDTH]

    c0 = 3 * A_WIDTH
    cq = _rms(h[:, c0:c0 + Q_LORA], gq_ref[...])
    q = jnp.dot(cq.astype(BF16), w_uq_ref[...], preferred_element_type=F32)
    q_parts = []
    for hh in range(MLA_HEADS):
        base = hh * MLA_KPAD
        q_parts.append(q[:, base:base + QK_NOPE])
        q_parts.append(rope(q[:, base + QK_NOPE:base + MLA_KPAD]))
    qm_ref[...] = (jnp.concatenate(q_parts, axis=1) * MLA_Q_SCALE).astype(BF16)

    c1 = c0 + Q_LORA
    ckv = _rms(h[:, c1:c1 + KV_LORA], gkv_ref[...])
    kv = jnp.dot(ckv.astype(BF16), w_ukv_ref[...], preferred_element_type=F32)
    c2 = c1 + KV_LORA
    k_pe = rope(h[:, c2:c2 + LANES])
    k_parts = []
    for hh in range(MLA_HEADS):
        k_parts.append(kv[:, hh * QK_NOPE:(hh + 1) * QK_NOPE])
        k_parts.append(k_pe)
    km_ref[...] = jnp.concatenate(k_parts, axis=1).astype(BF16)
    vm_ref[...] = kv[:, MLA_HEADS * QK_NOPE:].astype(BF16)


def _mix_ffn_head(a_ref, b_ref, x_ref, wo_ref, ga_ref, gb_ref, g1_ref, beta1_ref,
                  w1_ref, w2_ref, g2_ref, beta2_ref):
    mixed = jnp.concatenate([_rms(a_ref[...], ga_ref[...]), _rms(b_ref[...], gb_ref[...])], axis=1)
    y = jnp.dot(mixed.astype(BF16), wo_ref[...], preferred_element_type=F32)
    x1 = _layer_norm(ALPHA * x_ref[...] + y, g1_ref[...], beta1_ref[...])
    xb = x1.astype(BF16)
    acc = ALPHA * x1
    for c in range(D_FF // FF_CHUNK):
        cols = slice(c * FF_CHUNK, (c + 1) * FF_CHUNK)
        hdn = jnp.dot(xb, w1_ref[:, cols], preferred_element_type=F32)
        hdn = jnp.square(jnp.maximum(hdn, 0.0)).astype(BF16)
        acc = acc + jnp.dot(hdn, w2_ref[cols, :], preferred_element_type=F32)
    return _layer_norm(acc, g2_ref[...], beta2_ref[...])


def _permute_rows(src, o_ref, slab_sc, *, inverse):
    tm, width = o_ref.shape
    tiles = TOKEN_GROUP // SUBLANES
    if not inverse:
        for c in range(width // LANES):
            slab_sc[c] = src[:, c * LANES:(c + 1) * LANES]
    for g in range(tm // TOKEN_GROUP):
        for tile in range(tiles):
            first_token = 4 * (tile & 3) + (tile >> 2)
            natural = pl.ds(g * TOKEN_GROUP + first_token, SUBLANES, stride=tiles)
            p0 = g * TOKEN_GROUP + tile * SUBLANES
            for c in range(width // LANES):
                if inverse:
                    slab_sc[c, natural, :] = src[p0:p0 + SUBLANES, c * LANES:(c + 1) * LANES]
                else:
                    o_ref[p0:p0 + SUBLANES, c * LANES:(c + 1) * LANES] = slab_sc[c, natural, :]
    if inverse:
        for c in range(width // LANES):
            o_ref[:, c * LANES:(c + 1) * LANES] = slab_sc[c]


N_MIX_INPUTS = 12
N_IN_PROJ_INPUTS = 7


def _stage_kernel(*refs, first, last):
    refs = list(refs)
    if first:
        head_refs, refs = refs[:1], refs[1:]
    else:
        head_refs, refs = refs[:N_MIX_INPUTS], refs[N_MIX_INPUTS:]
    if not last:
        tail_in, refs = refs[:N_IN_PROJ_INPUTS], refs[N_IN_PROJ_INPUTS:]
    x_out_ref, refs = refs[0], refs[1:]
    if first:
        _permute_rows(head_refs[0], x_out_ref, refs[-1], inverse=False)
        x_new = x_out_ref[...]
    else:
        x_new = _mix_ffn_head(*head_refs)
        if last:
            _permute_rows(x_new, x_out_ref, refs[-1], inverse=True)
        else:
            x_out_ref[...] = x_new
    if not last:
        _in_proj_tail(x_new, *tail_in, *refs[:len(IN_PROJ_OUTS)])


def _stage(head_args, tail_args, *, first, last, seq, tm=STAGE_TM):
    x = head_args[0] if first else head_args[2]
    t = x.shape[0]
    row = lambda i: (i, 0)
    n_pos_blocks = seq // tm
    pos = lambda i: (i % n_pos_blocks, 0)
    if first:
        in_specs = [pl.BlockSpec((tm, D_MODEL), row)]
    else:
        in_specs = [pl.BlockSpec((tm, A_WIDTH), row), pl.BlockSpec((tm, MLA_WIDTH), row),
                    pl.BlockSpec((tm, D_MODEL), row)] + [_const_spec(c.shape) for c in head_args[3:]]
    out_specs = [pl.BlockSpec((tm, D_MODEL), row)]
    out_shape = [jax.ShapeDtypeStruct((t, D_MODEL), F32)]
    if not last:
        in_specs += [_const_spec(c.shape) for c in tail_args[:5]]
        in_specs += [pl.BlockSpec((tm, LANES), pos), pl.BlockSpec((tm, LANES), pos)]
        out_specs += [pl.BlockSpec((tm, w), row) for w, _ in IN_PROJ_OUTS]
        out_shape += [jax.ShapeDtypeStruct((t, w), dt) for w, dt in IN_PROJ_OUTS]
    scratch = [pltpu.VMEM((D_MODEL // LANES, tm, LANES), F32)] if (first or last) else []
    return pl.pallas_call(
        functools.partial(_stage_kernel, first=first, last=last),
        grid=(t // tm,),
        in_specs=in_specs, out_specs=out_specs, out_shape=out_shape, scratch_shapes=scratch,
        compiler_params=_params(1, STAGE_VMEM_LIMIT),
        name="stage_first" if first else ("stage_last" if last else "stage"),
    )(*head_args, *tail_args)


Q_TILES = 4


def _mla_kernel(q_ref, k_ref, v_ref, o_ref, m_sc, acc_sc, *, tq):
    first_diag = Q_TILES * pl.program_id(1)
    tk = tq
    m_sc[...] = jnp.full(m_sc.shape, NEG_INF, F32)
    acc_sc[...] = jnp.zeros(acc_sc.shape, F32)

    def tile(qt, k_tile, causal):
        k_start = pl.multiple_of(k_tile * tk, tk)
        for h in range(MLA_HEADS):
            q = q_ref[qt * tq:(qt + 1) * tq, h * MLA_KPAD:(h + 1) * MLA_KPAD]
            k = k_ref[pl.ds(k_start, tk), h * MLA_KPAD:(h + 1) * MLA_KPAD]
            v = v_ref[pl.ds(k_start, tk), h * V_DIM:(h + 1) * V_DIM]
            s = lax.dot_general(q, k, (((1,), (1,)), ((), ())), preferred_element_type=F32)
            if causal is not None:
                s = jnp.where(causal, s, NEG_INF)
            m_prev = m_sc[qt, h]
            m_new = jnp.maximum(m_prev, jnp.max(s, -1, keepdims=True))
            alpha = jnp.exp2(m_prev - m_new)
            p = jnp.exp2(s - jnp.concatenate([m_new] * (tk // LANES), axis=1))
            v1 = jnp.concatenate([v, jnp.ones_like(v)], axis=1)
            acc_sc[qt, h] = (jnp.concatenate([alpha, alpha], axis=1) * acc_sc[qt, h]
                             + jnp.dot(p.astype(BF16), v1, preferred_element_type=F32))
            m_sc[qt, h] = m_new

    def below_diagonal(k_tile, carry):
        for qt in range(Q_TILES):
            tile(qt, k_tile, None)
        return carry

    lax.fori_loop(0, first_diag, below_diagonal, 0)

    r = lax.broadcasted_iota(jnp.int32, (tq, tk), 0)
    c = lax.broadcasted_iota(jnp.int32, (tq, tk), 1)
    causal = ((c & -TOKEN_GROUP) + _position_in_group(c)) <= ((r & -TOKEN_GROUP) + _position_in_group(r))
    for qt in range(Q_TILES):
        for kt in range(qt + 1):
            tile(qt, first_diag + kt, causal if kt == qt else None)
    for qt in range(Q_TILES):
        for h in range(MLA_HEADS):
            acc = acc_sc[qt, h]
            o_ref[qt * tq:(qt + 1) * tq, h * V_DIM:(h + 1) * V_DIM] = acc[:, :V_DIM] / acc[:, V_DIM:]


def _mla_attention(qm, km, vm, *, batch, seq, tq):
    t = qm.shape[0]
    steps = seq // (Q_TILES * tq)
    q_map = lambda b, i: (b * steps + i, 0)
    kv_map = lambda b, i: (b, 0)
    return pl.pallas_call(
        functools.partial(_mla_kernel, tq=tq),
        grid=(batch, steps),
        in_specs=[
            pl.BlockSpec((Q_TILES * tq, MLA_HEADS * MLA_KPAD), q_map),
            pl.BlockSpec((seq, MLA_HEADS * MLA_KPAD), kv_map),
            pl.BlockSpec((seq, MLA_WIDTH), kv_map),
        ],
        out_specs=pl.BlockSpec((Q_TILES * tq, MLA_WIDTH), q_map),
        out_shape=jax.ShapeDtypeStruct((t, MLA_WIDTH), F32),
        scratch_shapes=[pltpu.VMEM((Q_TILES, MLA_HEADS, tq, LANES), F32),
                        pltpu.VMEM((Q_TILES, MLA_HEADS, tq, 2 * V_DIM), F32)],
        compiler_params=_params(2),
        name="mla_attention",
    )(qm, km, vm)


HEAD_PAIRS = A_WIDTH // LANES


def _sub_index(i, dilation):
    if dilation == 16:
        return i
    if dilation == 4:
        return 32 * (i >> 5) + 4 * (i & 7) + ((i >> 3) & 3)
    return _position_in_group(i)


def _band_bias(dilation, has_prev):
    i = lax.broadcasted_iota(jnp.int32, (2 * SPAN, 2 * SPAN), 0) & (SPAN - 1)
    j = lax.broadcasted_iota(jnp.int32, (2 * SPAN, 2 * SPAN), 1)
    dist = SPAN + _sub_index(i, dilation) - ((j & SPAN) + _sub_index(j & (SPAN - 1), dilation))
    valid = (dist >= 0) & (dist <= SPAN)
    if has_prev is not None:
        valid = valid & ((j >= SPAN) | has_prev)
    return jnp.where(valid, 0.0, NEG_INF)


def _dilated_kernel(q_ref, k_ref, v_ref, o_ref, kp_sc, vp_sc, m_sc, l_sc):
    @pl.when((pl.program_id(0) == 0) & (pl.program_id(1) == 0))
    def _():
        kp_sc[...] = jnp.zeros(kp_sc.shape, F32)
        vp_sc[...] = jnp.zeros(vp_sc.shape, F32)

    has_prev = pl.program_id(1) > 0
    lane = lax.broadcasted_iota(jnp.int32, (SPAN, LANES), 1)
    head_a = lane < A_HEAD_DIM
    ones = jnp.ones((2 * SPAN, LANES), BF16)

    def rows2d(t):
        return t.reshape(-1, LANES)

    def attend(q, k, v, bias, idx, first, last):
        qa = jnp.where(head_a, q, 0.0).astype(BF16)
        qb = jnp.where(head_a, 0.0, q).astype(BF16)
        q2 = jnp.concatenate([qa, qb], axis=0)
        s = lax.dot_general(q2, k.astype(BF16), (((1,), (1,)), ((), ())), preferred_element_type=F32) + bias
        m = jnp.max(s, -1, keepdims=True)
        p = jnp.exp2(s - m)
        pv = jnp.dot(p.astype(BF16), jnp.concatenate([v.astype(BF16), ones], axis=1),
                     preferred_element_type=F32)
        acc = jnp.where(head_a, pv[:SPAN, :LANES], pv[SPAN:, :LANES])
        l = jnp.where(head_a, pv[:SPAN, LANES:], pv[SPAN:, LANES:])
        m_cur = jnp.where(head_a, jnp.broadcast_to(m[:SPAN], (SPAN, LANES)),
                          jnp.broadcast_to(m[SPAN:], (SPAN, LANES)))
        shape = o_ref[idx].shape
        if first:
            o_ref[idx] = acc.reshape(shape)
            l_sc[idx] = l.reshape(shape)
            m_sc[idx] = m_cur.reshape(shape)
            return
        m_old = rows2d(m_sc[idx])
        m_new = jnp.maximum(m_old, m_cur)
        a_old = jnp.exp2(m_old - m_new)
        a_cur = jnp.exp2(m_cur - m_new)
        acc = a_old * rows2d(o_ref[idx]) + a_cur * acc
        l = a_old * rows2d(l_sc[idx]) + a_cur * l
        if last:
            o_ref[idx] = (acc / l).reshape(shape)
        else:
            o_ref[idx] = acc.reshape(shape)
            l_sc[idx] = l.reshape(shape)
            m_sc[idx] = m_new.reshape(shape)

    every = slice(None)

    band = _band_bias(1, None)
    band_first = _band_bias(1, has_prev)
    for p in range(HEAD_PAIRS):
        cols = pl.ds(p * LANES, LANES)
        k = jnp.concatenate([rows2d(kp_sc[CHUNK_GROUPS - 1, :, :, cols]), rows2d(k_ref[0, :, :, cols])], axis=0)
        v = jnp.concatenate([rows2d(vp_sc[CHUNK_GROUPS - 1, :, :, cols]), rows2d(v_ref[0, :, :, cols])], axis=0)
        attend(rows2d(q_ref[0, :, :, cols]), k, v, band_first, (0, every, every, cols), True, False)

    def d1_body(g, carry):
        for p in range(HEAD_PAIRS):
            cols = pl.ds(p * LANES, LANES)
            attend(rows2d(q_ref[g, :, :, cols]), rows2d(k_ref[pl.ds(g - 1, 2), :, :, cols]),
                   rows2d(v_ref[pl.ds(g - 1, 2), :, :, cols]), band, (g, every, every, cols), True, False)
        return carry
    lax.fori_loop(1, CHUNK_GROUPS, d1_body, 0, unroll=3)

    band = _band_bias(4, None)
    band_first = _band_bias(4, has_prev)

    def d4_first(r4, carry):
        res = pl.ds(4 * r4, 4)
        for p in range(HEAD_PAIRS):
            cols = pl.ds(p * LANES, LANES)
            k = jnp.concatenate([rows2d(kp_sc[CHUNK_GROUPS - 4:, res, :, cols]), rows2d(k_ref[0:4, res, :, cols])],
                                axis=0)
            v = jnp.concatenate([rows2d(vp_sc[CHUNK_GROUPS - 4:, res, :, cols]), rows2d(v_ref[0:4, res, :, cols])],
                                axis=0)
            attend(rows2d(q_ref[0:4, res, :, cols]), k, v, band_first, (slice(0, 4), res, every, cols), False, False)
        return carry
    lax.fori_loop(0, 4, d4_first, 0, unroll=4)

    blocks_after_first = CHUNK_GROUPS // 4 - 1

    def d4_body(t, carry):
        res = pl.ds(4 * (t // blocks_after_first), 4)
        g0 = 4 * (1 + t % blocks_after_first)
        for p in range(HEAD_PAIRS):
            cols = pl.ds(p * LANES, LANES)
            attend(rows2d(q_ref[pl.ds(g0, 4), res, :, cols]), rows2d(k_ref[pl.ds(g0 - 4, 8), res, :, cols]),
                   rows2d(v_ref[pl.ds(g0 - 4, 8), res, :, cols]), band, (pl.ds(g0, 4), res, every, cols),
                   False, False)
        return carry
    lax.fori_loop(0, 4 * blocks_after_first, d4_body, 0, unroll=4)

    band16 = _band_bias(16, has_prev)

    def d16_body(r, carry):
        for p in range(HEAD_PAIRS):
            cols = pl.ds(p * LANES, LANES)
            k = jnp.concatenate([rows2d(kp_sc[:, r, :, cols]), rows2d(k_ref[:, r, :, cols])], axis=0)
            v = jnp.concatenate([rows2d(vp_sc[:, r, :, cols]), rows2d(v_ref[:, r, :, cols])], axis=0)
            attend(rows2d(q_ref[:, r, :, cols]), k, v, band16, (every, r, every, cols), False, True)
        return carry
    lax.fori_loop(0, MAX_DILATION, d16_body, 0, unroll=4)

    kp_sc[...] = k_ref[...]
    vp_sc[...] = v_ref[...]


def _dilated_attention(qa, ka, va, *, batch, seq):
    t = qa.shape[0]
    chunks = seq // (CHUNK_GROUPS * TOKEN_GROUP)
    tiles = TOKEN_GROUP // SUBLANES
    shape4 = (t // TOKEN_GROUP, tiles, SUBLANES, A_WIDTH)
    view = lambda a: a.reshape(shape4)
    block = (CHUNK_GROUPS, tiles, SUBLANES, A_WIDTH)
    spec = pl.BlockSpec(block, lambda b, c: (b * chunks + c, 0, 0, 0))
    scratch = pltpu.VMEM(block, F32)
    out = pl.pallas_call(
        _dilated_kernel,
        grid=(batch, chunks),
        in_specs=[spec, spec, spec],
        out_specs=spec,
        out_shape=jax.ShapeDtypeStruct(shape4, F32),
        scratch_shapes=[scratch, scratch, scratch, scratch],
        compiler_params=_params(2),
        name="dilated",
    )(view(qa), view(ka), view(va))
    return out.reshape(t, A_WIDTH)


def _rope_tables(seq):
    half = A_HEAD_DIM // 2
    inv_freq = ROPE_THETA ** (-jnp.arange(half, dtype=F32) / half)
    row = jnp.arange(seq, dtype=jnp.int32)
    pos = (row & -TOKEN_GROUP) + _position_in_group(row)
    ang = pos.astype(F32)[:, None] * inv_freq[None, :]
    cos = jnp.cos(ang)
    sin = jnp.sin(ang)
    cos128 = jnp.tile(cos, (1, LANES // half))
    sin128 = jnp.tile(jnp.concatenate([-sin, sin], axis=1), (1, LANES // A_HEAD_DIM))
    return cos128, sin128


def _prep_w_uq(w):
    w = w.reshape(Q_LORA, MLA_HEADS, QK_NOPE + QK_ROPE)
    w = jnp.pad(w, ((0, 0), (0, 0), (0, MLA_KPAD - QK_NOPE - QK_ROPE)))
    return w.reshape(Q_LORA, MLA_HEADS * MLA_KPAD).astype(BF16)


def _prep_w_ukv(w):
    w = w.reshape(KV_LORA, MLA_HEADS, QK_NOPE + V_DIM)
    k = w[:, :, :QK_NOPE].reshape(KV_LORA, MLA_HEADS * QK_NOPE)
    v = w[:, :, QK_NOPE:].reshape(KV_LORA, MLA_HEADS * V_DIM)
    return jnp.concatenate([k, v], axis=1).astype(BF16)


def kernel(x, w_in, q_a_norm, kv_a_norm, w_uq, w_ukv, a_out_norm, b_out_norm, w_o,
           ln1_g, ln1_b, w_ff1, w_ff2, ln2_g, ln2_b):
    batch, seq, _ = x.shape
    assert seq % (CHUNK_GROUPS * TOKEN_GROUP) == 0
    t = batch * seq
    cos, sin = _rope_tables(seq)
    row2d = lambda v: v.reshape(1, -1)

    def in_proj_args(l):
        w_in_l = jnp.pad(w_in[l], ((0, 0), (0, IN_COLS_PAD - w_in.shape[2]))).astype(BF16)
        return (w_in_l, _prep_w_uq(w_uq[l]), _prep_w_ukv(w_ukv[l]), row2d(q_a_norm[l]), row2d(kv_a_norm[l]),
                cos, sin)

    def mix_args(l):
        return (w_o[l].astype(BF16), row2d(a_out_norm[l]), row2d(b_out_norm[l]), row2d(ln1_g[l]),
                row2d(ln1_b[l]), w_ff1[l].astype(BF16), w_ff2[l].astype(BF16), row2d(ln2_g[l]), row2d(ln2_b[l]))

    xf, qa, ka, va, qm, km, vm = _stage((x.reshape(t, D_MODEL),), in_proj_args(0),
                                        first=True, last=False, seq=seq)
    for l in range(DEPTH):
        b_out = _mla_attention(qm, km, vm, batch=batch, seq=seq, tq=512)
        a_out = _dilated_attention(qa, ka, va, batch=batch, seq=seq)
        head = (a_out, b_out, xf) + mix_args(l)
        if l + 1 < DEPTH:
            xf, qa, ka, va, qm, km, vm = _stage(head, in_proj_args(l + 1), first=False, last=False, seq=seq)
        else:
            (xf,) = _stage(head, (), first=False, last=True, seq=seq)
    return xf.reshape(batch, seq, D_MODEL)
```

```python
import functools

import jax
import jax.numpy as jnp
from jax import lax
from jax.experimental import pallas as pl
from jax.experimental.pallas import tpu as pltpu

D_MODEL = 1024
DEPTH = 4
A_HEADS = 8
A_HEAD_DIM = 64
A_WIDTH = A_HEADS * A_HEAD_DIM
MLA_HEADS = 4
QK_NOPE = 128
QK_ROPE = 64
V_DIM = 128
Q_LORA = 256
KV_LORA = 128
MLA_WIDTH = MLA_HEADS * V_DIM
D_FF = 4 * D_MODEL
ROPE_THETA = 10000.0
ALPHA = (2.0 * DEPTH) ** 0.25
LN_EPS = 1e-5
RMS_EPS = 1e-6

LANES = 128
SUBLANES = 8
MLA_KPAD = 256
IN_COLS_PAD = 3 * A_WIDTH + Q_LORA + KV_LORA + LANES
VMEM_LIMIT = 56 * 1024 * 1024
STAGE_VMEM_LIMIT = 60 * 1024 * 1024
STAGE_TM = 512

SPAN = 128
MAX_DILATION = 16
TOKEN_GROUP = SPAN
CHUNK_GROUPS = SPAN * MAX_DILATION // TOKEN_GROUP

F32 = jnp.float32
BF16 = jnp.bfloat16
NEG_INF = float("-inf")
LOG2E = 1.4426950408889634
MLA_Q_SCALE = (QK_NOPE + QK_ROPE) ** -0.5 * LOG2E
A_Q_SCALE = A_HEAD_DIM ** -0.5 * LOG2E


def _const_spec(shape):
    return pl.BlockSpec(shape, lambda *_: (0,) * len(shape), pipeline_mode=pl.Buffered(1))


def _params(n_axes, vmem_limit=VMEM_LIMIT):
    return pltpu.CompilerParams(dimension_semantics=("arbitrary",) * n_axes,
                                vmem_limit_bytes=vmem_limit)


def _rms(x, g):
    return x * lax.rsqrt(jnp.mean(x * x, -1, keepdims=True) + RMS_EPS) * g


def _layer_norm(z, g, b):
    mu = jnp.mean(z, -1, keepdims=True)
    zc = z - mu
    var = jnp.mean(zc * zc, -1, keepdims=True)
    return zc * lax.rsqrt(var + LN_EPS) * g + b


def _rope128(x, cos, sin_signed, first_half):
    rot = jnp.where(first_half, pltpu.roll(x, 96, 1), pltpu.roll(x, 32, 1))
    return x * cos + rot * sin_signed


def _position_in_group(i):
    return 16 * (i & 7) + 4 * ((i >> 3) & 3) + ((i >> 5) & 3)


FF_CHUNK = 1024
IN_PROJ_OUTS = ((A_WIDTH, F32), (A_WIDTH, F32), (A_WIDTH, F32),
                (MLA_HEADS * MLA_KPAD, BF16), (MLA_HEADS * MLA_KPAD, BF16), (MLA_WIDTH, BF16))


def _in_proj_tail(x, w_in_ref, w_uq_ref, w_ukv_ref, gq_ref, gkv_ref, cos_ref, sin_ref,
                  qa_ref, ka_ref, va_ref, qm_ref, km_ref, vm_ref):
    h = jnp.dot(x.astype(BF16), w_in_ref[...], preferred_element_type=F32)
    cos = cos_ref[...]
    sin = sin_ref[...]
    lane = lax.broadcasted_iota(jnp.int32, cos.shape, 1)
    first_half = (lane & (A_HEAD_DIM - 1)) < (A_HEAD_DIM // 2)

    def rope(t):
        return _rope128(t, cos, sin, first_half)

    def rope_cols(t):
        return jnp.concatenate(
            [rope(t[:, c * LANES:(c + 1) * LANES]) for c in range(t.shape[1] // LANES)], axis=1)

    qa_ref[...] = rope_cols(h[:, 0:A_WIDTH]) * A_Q_SCALE
    ka_ref[...] = rope_cols(h[:, A_WIDTH:2 * A_WIDTH])
    va_ref[...] = h[:, 2 * A_WIDTH:3 * A_WIDTH]

    c0 = 3 * A_WIDTH
    cq = _rms(h[:, c0:c0 + Q_LORA], gq_ref[...])
    q = jnp.dot(cq.astype(BF16), w_uq_ref[...], preferred_element_type=F32)
    q_parts = []
    for hh in range(MLA_HEADS):
        base = hh * MLA_KPAD
        q_parts.append(q[:, base:base + QK_NOPE])
        q_parts.append(rope(q[:, base + QK_NOPE:base + MLA_KPAD]))
    qm_ref[...] = (jnp.concatenate(q_parts, axis=1) * MLA_Q_SCALE).astype(BF16)

    c1 = c0 + Q_LORA
    ckv = _rms(h[:, c1:c1 + KV_LORA], gkv_ref[...])
    kv = jnp.dot(ckv.astype(BF16), w_ukv_ref[...], preferred_element_type=F32)
    c2 = c1 + KV_LORA
    k_pe = rope(h[:, c2:c2 + LANES])
    k_parts = []
    for hh in range(MLA_HEADS):
        k_parts.append(kv[:, hh * QK_NOPE:(hh + 1) * QK_NOPE])
        k_parts.append(k_pe)
    km_ref[...] = jnp.concatenate(k_parts, axis=1).astype(BF16)
    vm_ref[...] = kv[:, MLA_HEADS * QK_NOPE:].astype(BF16)


def _mix_ffn_head(a_ref, b_ref, x_ref, wo_ref, ga_ref, gb_ref, g1_ref, beta1_ref,
                  w1_ref, w2_ref, g2_ref, beta2_ref):
    mixed = jnp.concatenate([_rms(a_ref[...], ga_ref[...]), _rms(b_ref[...], gb_ref[...])], axis=1)
    y = jnp.dot(mixed.astype(BF16), wo_ref[...], preferred_element_type=F32)
    x1 = _layer_norm(ALPHA * x_ref[...] + y, g1_ref[...], beta1_ref[...])
    xb = x1.astype(BF16)
    acc = ALPHA * x1
    for c in range(D_FF // FF_CHUNK):
        cols = slice(c * FF_CHUNK, (c + 1) * FF_CHUNK)
        hdn = jnp.dot(xb, w1_ref[:, cols], preferred_element_type=F32)
        hdn = jnp.square(jnp.maximum(hdn, 0.0)).astype(BF16)
        acc = acc + jnp.dot(hdn, w2_ref[cols, :], preferred_element_type=F32)
    return _layer_norm(acc, g2_ref[...], beta2_ref[...])


def _permute_rows(src, o_ref, slab_sc, *, inverse):
    tm, width = o_ref.shape
    tiles = TOKEN_GROUP // SUBLANES
    if not inverse:
        for c in range(width // LANES):
            slab_sc[c] = src[:, c * LANES:(c + 1) * LANES]
    for g in range(tm // TOKEN_GROUP):
        for tile in range(tiles):
            first_token = 4 * (tile & 3) + (tile >> 2)
            natural = pl.ds(g * TOKEN_GROUP + first_token, SUBLANES, stride=tiles)
            p0 = g * TOKEN_GROUP + tile * SUBLANES
            for c in range(width // LANES):
                if inverse:
                    slab_sc[c, natural, :] = src[p0:p0 + SUBLANES, c * LANES:(c + 1) * LANES]
                else:
                    o_ref[p0:p0 + SUBLANES, c * LANES:(c + 1) * LANES] = slab_sc[c, natural, :]
    if inverse:
        for c in range(width // LANES):
            o_ref[:, c * LANES:(c + 1) * LANES] = slab_sc[c]


N_MIX_INPUTS = 12
N_IN_PROJ_INPUTS = 7


def _stage_kernel(*refs, first, last):
    refs = list(refs)
    if first:
        head_refs, refs = refs[:1], refs[1:]
    else:
        head_refs, refs = refs[:N_MIX_INPUTS], refs[N_MIX_INPUTS:]
    if not last:
        tail_in, refs = refs[:N_IN_PROJ_INPUTS], refs[N_IN_PROJ_INPUTS:]
    x_out_ref, refs = refs[0], refs[1:]
    if first:
        _permute_rows(head_refs[0], x_out_ref, refs[-1], inverse=False)
        x_new = x_out_ref[...]
    else:
        x_new = _mix_ffn_head(*head_refs)
        if last:
            _permute_rows(x_new, x_out_ref, refs[-1], inverse=True)
        else:
            x_out_ref[...] = x_new
    if not last:
        _in_proj_tail(x_new, *tail_in, *refs[:len(IN_PROJ_OUTS)])


def _stage(head_args, tail_args, *, first, last, seq, tm=STAGE_TM):
    x = head_args[0] if first else head_args[2]
    t = x.shape[0]
    row = lambda i: (i, 0)
    n_pos_blocks = seq // tm
    pos = lambda i: (i % n_pos_blocks, 0)
    if first:
        in_specs = [pl.BlockSpec((tm, D_MODEL), row)]
    else:
        in_specs = [pl.BlockSpec((tm, A_WIDTH), row), pl.BlockSpec((tm, MLA_WIDTH), row),
                    pl.BlockSpec((tm, D_MODEL), row)] + [_const_spec(c.shape) for c in head_args[3:]]
    out_specs = [pl.BlockSpec((tm, D_MODEL), row)]
    out_shape = [jax.ShapeDtypeStruct((t, D_MODEL), F32)]
    if not last:
        in_specs += [_const_spec(c.shape) for c in tail_args[:5]]
        in_specs += [pl.BlockSpec((tm, LANES), pos), pl.BlockSpec((tm, LANES), pos)]
        out_specs += [pl.BlockSpec((tm, w), row) for w, _ in IN_PROJ_OUTS]
        out_shape += [jax.ShapeDtypeStruct((t, w), dt) for w, dt in IN_PROJ_OUTS]
    scratch = [pltpu.VMEM((D_MODEL // LANES, tm, LANES), F32)] if (first or last) else []
    return pl.pallas_call(
        functools.partial(_stage_kernel, first=first, last=last),
        grid=(t // tm,),
        in_specs=in_specs, out_specs=out_specs, out_shape=out_shape, scratch_shapes=scratch,
        compiler_params=_params(1, STAGE_VMEM_LIMIT),
        name="stage_first" if first else ("stage_last" if last else "stage"),
    )(*head_args, *tail_args)


Q_TILES = 4


def _mla_kernel(q_ref, k_ref, v_ref, o_ref, m_sc, acc_sc, *, tq):
    first_diag = Q_TILES * pl.program_id(1)
    tk = tq
    m_sc[...] = jnp.full(m_sc.shape, NEG_INF, F32)
    acc_sc[...] = jnp.zeros(acc_sc.shape, F32)

    def tile(qt, k_tile, causal):
        k_start = pl.multiple_of(k_tile * tk, tk)
        for h in range(MLA_HEADS):
            q = q_ref[qt * tq:(qt + 1) * tq, h * MLA_KPAD:(h + 1) * MLA_KPAD]
            k = k_ref[pl.ds(k_start, tk), h * MLA_KPAD:(h + 1) * MLA_KPAD]
            v = v_ref[pl.ds(k_start, tk), h * V_DIM:(h + 1) * V_DIM]
            s = lax.dot_general(q, k, (((1,), (1,)), ((), ())), preferred_element_type=F32)
            if causal is not None:
                s = jnp.where(causal, s, NEG_INF)
            m_prev = m_sc[qt, h]
            m_new = jnp.maximum(m_prev, jnp.max(s, -1, keepdims=True))
            alpha = jnp.exp2(m_prev - m_new)
            p = jnp.exp2(s - jnp.concatenate([m_new] * (tk // LANES), axis=1))
            v1 = jnp.concatenate([v, jnp.ones_like(v)], axis=1)
            acc_sc[qt, h] = (jnp.concatenate([alpha, alpha], axis=1) * acc_sc[qt, h]
                             + jnp.dot(p.astype(BF16), v1, preferred_element_type=F32))
            m_sc[qt, h] = m_new

    def below_diagonal(k_tile, carry):
        for qt in range(Q_TILES):
            tile(qt, k_tile, None)
        return carry

    lax.fori_loop(0, first_diag, below_diagonal, 0)

    r = lax.broadcasted_iota(jnp.int32, (tq, tk), 0)
    c = lax.broadcasted_iota(jnp.int32, (tq, tk), 1)
    causal = ((c & -TOKEN_GROUP) + _position_in_group(c)) <= ((r & -TOKEN_GROUP) + _position_in_group(r))
    for qt in range(Q_TILES):
        for kt in range(qt + 1):
            tile(qt, first_diag + kt, causal if kt == qt else None)
    for qt in range(Q_TILES):
        for h in range(MLA_HEADS):
            acc = acc_sc[qt, h]
            o_ref[qt * tq:(qt + 1) * tq, h * V_DIM:(h + 1) * V_DIM] = acc[:, :V_DIM] / acc[:, V_DIM:]


def _mla_attention(qm, km, vm, *, batch, seq, tq):
    t = qm.shape[0]
    steps = seq // (Q_TILES * tq)
    q_map = lambda b, i: (b * steps + i, 0)
    kv_map = lambda b, i: (b, 0)
    return pl.pallas_call(
        functools.partial(_mla_kernel, tq=tq),
        grid=(batch, steps),
        in_specs=[
            pl.BlockSpec((Q_TILES * tq, MLA_HEADS * MLA_KPAD), q_map),
            pl.BlockSpec((seq, MLA_HEADS * MLA_KPAD), kv_map),
            pl.BlockSpec((seq, MLA_WIDTH), kv_map),
        ],
        out_specs=pl.BlockSpec((Q_TILES * tq, MLA_WIDTH), q_map),
        out_shape=jax.ShapeDtypeStruct((t, MLA_WIDTH), F32),
        scratch_shapes=[pltpu.VMEM((Q_TILES, MLA_HEADS, tq, LANES), F32),
                        pltpu.VMEM((Q_TILES, MLA_HEADS, tq, 2 * V_DIM), F32)],
        compiler_params=_params(2),
        name="mla_attention",
    )(qm, km, vm)


HEAD_PAIRS = A_WIDTH // LANES


def _sub_index(i, dilation):
    if dilation == 16:
        return i
    if dilation == 4:
        return 32 * (i >> 5) + 4 * (i & 7) + ((i >> 3) & 3)
    return _position_in_group(i)


def _band_bias(dilation, has_prev):
    i = lax.broadcasted_iota(jnp.int32, (2 * SPAN, 2 * SPAN), 0) & (SPAN - 1)
    j = lax.broadcasted_iota(jnp.int32, (2 * SPAN, 2 * SPAN), 1)
    dist = SPAN + _sub_index(i, dilation) - ((j & SPAN) + _sub_index(j & (SPAN - 1), dilation))
    valid = (dist >= 0) & (dist <= SPAN)
    if has_prev is not None:
        valid = valid & ((j >= SPAN) | has_prev)
    return jnp.where(valid, 0.0, NEG_INF)


def _dilated_kernel(q_ref, k_ref, v_ref, o_ref, kp_sc, vp_sc, m_sc, l_sc):
    @pl.when((pl.program_id(0) == 0) & (pl.program_id(1) == 0))
    def _():
        kp_sc[...] = jnp.zeros(kp_sc.shape, F32)
        vp_sc[...] = jnp.zeros(vp_sc.shape, F32)

    has_prev = pl.program_id(1) > 0
    lane = lax.broadcasted_iota(jnp.int32, (SPAN, LANES), 1)
    head_a = lane < A_HEAD_DIM
    ones = jnp.ones((2 * SPAN, LANES), BF16)

    def rows2d(t):
        return t.reshape(-1, LANES)

    def attend(q, k, v, bias, idx, first, last):
        qa = jnp.where(head_a, q, 0.0).astype(BF16)
        qb = jnp.where(head_a, 0.0, q).astype(BF16)
        q2 = jnp.concatenate([qa, qb], axis=0)
        s = lax.dot_general(q2, k.astype(BF16), (((1,), (1,)), ((), ())), preferred_element_type=F32) + bias
        m = jnp.max(s, -1, keepdims=True)
        p = jnp.exp2(s - m)
        pv = jnp.dot(p.astype(BF16), jnp.concatenate([v.astype(BF16), ones], axis=1),
                     preferred_element_type=F32)
        acc = jnp.where(head_a, pv[:SPAN, :LANES], pv[SPAN:, :LANES])
        l = jnp.where(head_a, pv[:SPAN, LANES:], pv[SPAN:, LANES:])
        m_cur = jnp.where(head_a, jnp.broadcast_to(m[:SPAN], (SPAN, LANES)),
                          jnp.broadcast_to(m[SPAN:], (SPAN, LANES)))
        shape = o_ref[idx].shape
        if first:
            o_ref[idx] = acc.reshape(shape)
            l_sc[idx] = l.reshape(shape)
            m_sc[idx] = m_cur.reshape(shape)
            return
        m_old = rows2d(m_sc[idx])
        m_new = jnp.maximum(m_old, m_cur)
        a_old = jnp.exp2(m_old - m_new)
        a_cur = jnp.exp2(m_cur - m_new)
        acc = a_old * rows2d(o_ref[idx]) + a_cur * acc
        l = a_old * rows2d(l_sc[idx]) + a_cur * l
        if last:
            o_ref[idx] = (acc / l).reshape(shape)
        else:
            o_ref[idx] = acc.reshape(shape)
            l_sc[idx] = l.reshape(shape)
            m_sc[idx] = m_new.reshape(shape)

    every = slice(None)

    band = _band_bias(1, None)
    band_first = _band_bias(1, has_prev)
    for p in range(HEAD_PAIRS):
        cols = pl.ds(p * LANES, LANES)
        k = jnp.concatenate([rows2d(kp_sc[CHUNK_GROUPS - 1, :, :, cols]), rows2d(k_ref[0, :, :, cols])], axis=0)
        v = jnp.concatenate([rows2d(vp_sc[CHUNK_GROUPS - 1, :, :, cols]), rows2d(v_ref[0, :, :, cols])], axis=0)
        attend(rows2d(q_ref[0, :, :, cols]), k, v, band_first, (0, every, every, cols), True, False)

    def d1_body(g, carry):
        for p in range(HEAD_PAIRS):
            cols = pl.ds(p * LANES, LANES)
            attend(rows2d(q_ref[g, :, :, cols]), rows2d(k_ref[pl.ds(g - 1, 2), :, :, cols]),
                   rows2d(v_ref[pl.ds(g - 1, 2), :, :, cols]), band, (g, every, every, cols), True, False)
        return carry
    lax.fori_loop(1, CHUNK_GROUPS, d1_body, 0, unroll=5)

    band = _band_bias(4, None)
    band_first = _band_bias(4, has_prev)

    def d4_first(r4, carry):
        res = pl.ds(4 * r4, 4)
        for p in range(HEAD_PAIRS):
            cols = pl.ds(p * LANES, LANES)
            k = jnp.concatenate([rows2d(kp_sc[CHUNK_GROUPS - 4:, res, :, cols]), rows2d(k_ref[0:4, res, :, cols])],
                                axis=0)
            v = jnp.concatenate([rows2d(vp_sc[CHUNK_GROUPS - 4:, res, :, cols]), rows2d(v_ref[0:4, res, :, cols])],
                                axis=0)
            attend(rows2d(q_ref[0:4, res, :, cols]), k, v, band_first, (slice(0, 4), res, every, cols), False, False)
        return carry
    lax.fori_loop(0, 4, d4_first, 0, unroll=4)

    blocks_after_first = CHUNK_GROUPS // 4 - 1

    def d4_body(t, carry):
        res = pl.ds(4 * (t // blocks_after_first), 4)
        g0 = 4 * (1 + t % blocks_after_first)
        for p in range(HEAD_PAIRS):
            cols = pl.ds(p * LANES, LANES)
            attend(rows2d(q_ref[pl.ds(g0, 4), res, :, cols]), rows2d(k_ref[pl.ds(g0 - 4, 8), res, :, cols]),
                   rows2d(v_ref[pl.ds(g0 - 4, 8), res, :, cols]), band, (pl.ds(g0, 4), res, every, cols),
                   False, False)
        return carry
    lax.fori_loop(0, 4 * blocks_after_first, d4_body, 0, unroll=6)

    band16 = _band_bias(16, has_prev)

    def d16_body(r, carry):
        for p in range(HEAD_PAIRS):
            cols = pl.ds(p * LANES, LANES)
            k = jnp.concatenate([rows2d(kp_sc[:, r, :, cols]), rows2d(k_ref[:, r, :, cols])], axis=0)
            v = jnp.concatenate([rows2d(vp_sc[:, r, :, cols]), rows2d(v_ref[:, r, :, cols])], axis=0)
            attend(rows2d(q_ref[:, r, :, cols]), k, v, band16, (every, r, every, cols), False, True)
        return carry
    lax.fori_loop(0, MAX_DILATION, d16_body, 0, unroll=8)

    kp_sc[...] = k_ref[...]
    vp_sc[...] = v_ref[...]


def _dilated_attention(qa, ka, va, *, batch, seq):
    t = qa.shape[0]
    chunks = seq // (CHUNK_GROUPS * TOKEN_GROUP)
    tiles = TOKEN_GROUP // SUBLANES
    shape4 = (t // TOKEN_GROUP, tiles, SUBLANES, A_WIDTH)
    view = lambda a: a.reshape(shape4)
    block = (CHUNK_GROUPS, tiles, SUBLANES, A_WIDTH)
    spec = pl.BlockSpec(block, lambda b, c: (b * chunks + c, 0, 0, 0))
    scratch = pltpu.VMEM(block, F32)
    out = pl.pallas_call(
        _dilated_kernel,
        grid=(batch, chunks),
        in_specs=[spec, spec, spec],
        out_specs=spec,
        out_shape=jax.ShapeDtypeStruct(shape4, F32),
        scratch_shapes=[scratch, scratch, scratch, scratch],
        compiler_params=_params(2),
        name="dilated",
    )(view(qa), view(ka), view(va))
    return out.reshape(t, A_WIDTH)


def _rope_tables(seq):
    half = A_HEAD_DIM // 2
    inv_freq = ROPE_THETA ** (-jnp.arange(half, dtype=F32) / half)
    row = jnp.arange(seq, dtype=jnp.int32)
    pos = (row & -TOKEN_GROUP) + _position_in_group(row)
    ang = pos.astype(F32)[:, None] * inv_freq[None, :]
    cos = jnp.cos(ang)
    sin = jnp.sin(ang)
    cos128 = jnp.tile(cos, (1, LANES // half))
    sin128 = jnp.tile(jnp.concatenate([-sin, sin], axis=1), (1, LANES // A_HEAD_DIM))
    return cos128, sin128


def _prep_w_uq(w):
    w = w.reshape(Q_LORA, MLA_HEADS, QK_NOPE + QK_ROPE)
    w = jnp.pad(w, ((0, 0), (0, 0), (0, MLA_KPAD - QK_NOPE - QK_ROPE)))
    return w.reshape(Q_LORA, MLA_HEADS * MLA_KPAD).astype(BF16)


def _prep_w_ukv(w):
    w = w.reshape(KV_LORA, MLA_HEADS, QK_NOPE + V_DIM)
    k = w[:, :, :QK_NOPE].reshape(KV_LORA, MLA_HEADS * QK_NOPE)
    v = w[:, :, QK_NOPE:].reshape(KV_LORA, MLA_HEADS * V_DIM)
    return jnp.concatenate([k, v], axis=1).astype(BF16)


def kernel(x, w_in, q_a_norm, kv_a_norm, w_uq, w_ukv, a_out_norm, b_out_norm, w_o,
           ln1_g, ln1_b, w_ff1, w_ff2, ln2_g, ln2_b):
    batch, seq, _ = x.shape
    assert seq % (CHUNK_GROUPS * TOKEN_GROUP) == 0
    t = batch * seq
    cos, sin = _rope_tables(seq)
    row2d = lambda v: v.reshape(1, -1)

    def in_proj_args(l):
        w_in_l = jnp.pad(w_in[l], ((0, 0), (0, IN_COLS_PAD - w_in.shape[2]))).astype(BF16)
        return (w_in_l, _prep_w_uq(w_uq[l]), _prep_w_ukv(w_ukv[l]), row2d(q_a_norm[l]), row2d(kv_a_norm[l]),
                cos, sin)

    def mix_args(l):
        return (w_o[l].astype(BF16), row2d(a_out_norm[l]), row2d(b_out_norm[l]), row2d(ln1_g[l]),
                row2d(ln1_b[l]), w_ff1[l].astype(BF16), w_ff2[l].astype(BF16), row2d(ln2_g[l]), row2d(ln2_b[l]))

    xf, qa, ka, va, qm, km, vm = _stage((x.reshape(t, D_MODEL),), in_proj_args(0),
                                        first=True, last=False, seq=seq)
    for l in range(DEPTH):
        b_out = _mla_attention(qm, km, vm, batch=batch, seq=seq, tq=512)
        a_out = _dilated_attention(qa, ka, va, batch=batch, seq=seq)
        head = (a_out, b_out, xf) + mix_args(l)
        if l + 1 < DEPTH:
            xf, qa, ka, va, qm, km, vm = _stage(head, in_proj_args(l + 1), first=False, last=False, seq=seq)
        else:
            (xf,) = _stage(head, (), first=False, last=True, seq=seq)
    return xf.reshape(batch, seq, D_MODEL)
```

```python
import functools

import jax
import jax.numpy as jnp
from jax import lax
from jax.experimental import pallas as pl
from jax.experimental.pallas import tpu as pltpu

D_MODEL = 1024
DEPTH = 4
A_HEADS = 8
A_HEAD_DIM = 64
A_WIDTH = A_HEADS * A_HEAD_DIM
MLA_HEADS = 4
QK_NOPE = 128
QK_ROPE = 64
V_DIM = 128
Q_LORA = 256
KV_LORA = 128
MLA_WIDTH = MLA_HEADS * V_DIM
D_FF = 4 * D_MODEL
ROPE_THETA = 10000.0
ALPHA = (2.0 * DEPTH) ** 0.25
LN_EPS = 1e-5
RMS_EPS = 1e-6

LANES = 128
SUBLANES = 8
MLA_KPAD = 256
IN_COLS_PAD = 3 * A_WIDTH + Q_LORA + KV_LORA + LANES
VMEM_LIMIT = 56 * 1024 * 1024
STAGE_VMEM_LIMIT = 60 * 1024 * 1024
STAGE_TM = 512

SPAN = 128
MAX_DILATION = 16
TOKEN_GROUP = SPAN
CHUNK_GROUPS = SPAN * MAX_DILATION // TOKEN_GROUP

F32 = jnp.float32
BF16 = jnp.bfloat16
NEG_INF = float("-inf")
LOG2E = 1.4426950408889634
MLA_Q_SCALE = (QK_NOPE + QK_ROPE) ** -0.5 * LOG2E
A_Q_SCALE = A_HEAD_DIM ** -0.5 * LOG2E


def _const_spec(shape):
    return pl.BlockSpec(shape, lambda *_: (0,) * len(shape), pipeline_mode=pl.Buffered(1))


def _params(n_axes, vmem_limit=VMEM_LIMIT):
    return pltpu.CompilerParams(dimension_semantics=("arbitrary",) * n_axes,
                                vmem_limit_bytes=vmem_limit)


def _rms(x, g):
    return x * lax.rsqrt(jnp.mean(x * x, -1, keepdims=True) + RMS_EPS) * g


def _layer_norm(z, g, b):
    mu = jnp.mean(z, -1, keepdims=True)
    zc = z - mu
    var = jnp.mean(zc * zc, -1, keepdims=True)
    return zc * lax.rsqrt(var + LN_EPS) * g + b


def _rope128(x, cos, sin_signed, first_half):
    rot = jnp.where(first_half, pltpu.roll(x, 96, 1), pltpu.roll(x, 32, 1))
    return x * cos + rot * sin_signed


def _position_in_group(i):
    return 16 * (i & 7) + 4 * ((i >> 3) & 3) + ((i >> 5) & 3)


FF_CHUNK = 1024
IN_PROJ_OUTS = ((A_WIDTH, F32), (A_WIDTH, F32), (A_WIDTH, F32),
                (MLA_HEADS * MLA_KPAD, BF16), (MLA_HEADS * MLA_KPAD, BF16), (MLA_WIDTH, BF16))


def _in_proj_tail(x, w_in_ref, w_uq_ref, w_ukv_ref, gq_ref, gkv_ref, cos_ref, sin_ref,
                  qa_ref, ka_ref, va_ref, qm_ref, km_ref, vm_ref):
    h = jnp.dot(x.astype(BF16), w_in_ref[...], preferred_element_type=F32)
    cos = cos_ref[...]
    sin = sin_ref[...]
    lane = lax.broadcasted_iota(jnp.int32, cos.shape, 1)
    first_half = (lane & (A_HEAD_DIM - 1)) < (A_HEAD_DIM // 2)

    def rope(t):
        return _rope128(t, cos, sin, first_half)

    def rope_cols(t):
        return jnp.concatenate(
            [rope(t[:, c * LANES:(c + 1) * LANES]) for c in range(t.shape[1] // LANES)], axis=1)

    qa_ref[...] = rope_cols(h[:, 0:A_WIDTH]) * A_Q_SCALE
    ka_ref[...] = rope_cols(h[:, A_WIDTH:2 * A_WIDTH])
    va_ref[...] = h[:, 2 * A_WIDTH:3 * A_WIDTH]

    c0 = 3 * A_WIDTH
    cq = _rms(h[:, c0:c0 + Q_LORA], gq_ref[...])
    q = jnp.dot(cq.astype(BF16), w_uq_ref[...], preferred_element_type=F32)
    q_parts = []
    for hh in range(MLA_HEADS):
        base = hh * MLA_KPAD
        q_parts.append(q[:, base:base + QK_NOPE])
        q_parts.append(rope(q[:, base + QK_NOPE:base + MLA_KPAD]))
    qm_ref[...] = (jnp.concatenate(q_parts, axis=1) * MLA_Q_SCALE).astype(BF16)

    c1 = c0 + Q_LORA
    ckv = _rms(h[:, c1:c1 + KV_LORA], gkv_ref[...])
    kv = jnp.dot(ckv.astype(BF16), w_ukv_ref[...], preferred_element_type=F32)
    c2 = c1 + KV_LORA
    k_pe = rope(h[:, c2:c2 + LANES])
    k_parts = []
    for hh in range(MLA_HEADS):
        k_parts.append(kv[:, hh * QK_NOPE:(hh + 1) * QK_NOPE])
        k_parts.append(k_pe)
    km_ref[...] = jnp.concatenate(k_parts, axis=1).astype(BF16)
    vm_ref[...] = kv[:, MLA_HEADS * QK_NOPE:].astype(BF16)


def _mix_ffn_head(a_ref, b_ref, x_ref, wo_ref, ga_ref, gb_ref, g1_ref, beta1_ref,
                  w1_ref, w2_ref, g2_ref, beta2_ref):
    mixed = jnp.concatenate([_rms(a_ref[...], ga_ref[...]), _rms(b_ref[...], gb_ref[...])], axis=1)
    y = jnp.dot(mixed.astype(BF16), wo_ref[...], preferred_element_type=F32)
    x1 = _layer_norm(ALPHA * x_ref[...] + y, g1_ref[...], beta1_ref[...])
    xb = x1.astype(BF16)
    acc = ALPHA * x1
    for c in range(D_FF // FF_CHUNK):
        cols = slice(c * FF_CHUNK, (c + 1) * FF_CHUNK)
        hdn = jnp.dot(xb, w1_ref[:, cols], preferred_element_type=F32)
        hdn = jnp.square(jnp.maximum(hdn, 0.0)).astype(BF16)
        acc = acc + jnp.dot(hdn, w2_ref[cols, :], preferred_element_type=F32)
    return _layer_norm(acc, g2_ref[...], beta2_ref[...])


def _permute_rows(src, o_ref, slab_sc, *, inverse):
    tm, width = o_ref.shape
    tiles = TOKEN_GROUP // SUBLANES
    if not inverse:
        for c in range(width // LANES):
            slab_sc[c] = src[:, c * LANES:(c + 1) * LANES]
    for g in range(tm // TOKEN_GROUP):
        for tile in range(tiles):
            first_token = 4 * (tile & 3) + (tile >> 2)
            natural = pl.ds(g * TOKEN_GROUP + first_token, SUBLANES, stride=tiles)
            p0 = g * TOKEN_GROUP + tile * SUBLANES
            for c in range(width // LANES):
                if inverse:
                    slab_sc[c, natural, :] = src[p0:p0 + SUBLANES, c * LANES:(c + 1) * LANES]
                else:
                    o_ref[p0:p0 + SUBLANES, c * LANES:(c + 1) * LANES] = slab_sc[c, natural, :]
    if inverse:
        for c in range(width // LANES):
            o_ref[:, c * LANES:(c + 1) * LANES] = slab_sc[c]


N_MIX_INPUTS = 12
N_IN_PROJ_INPUTS = 7


def _stage_kernel(*refs, first, last):
    refs = list(refs)
    if first:
        head_refs, refs = refs[:1], refs[1:]
    else:
        head_refs, refs = refs[:N_MIX_INPUTS], refs[N_MIX_INPUTS:]
    if not last:
        tail_in, refs = refs[:N_IN_PROJ_INPUTS], refs[N_IN_PROJ_INPUTS:]
    x_out_ref, refs = refs[0], refs[1:]
    if first:
        _permute_rows(head_refs[0], x_out_ref, refs[-1], inverse=False)
        x_new = x_out_ref[...]
    else:
        x_new = _mix_ffn_head(*head_refs)
        if last:
            _permute_rows(x_new, x_out_ref, refs[-1], inverse=True)
        else:
            x_out_ref[...] = x_new
    if not last:
        _in_proj_tail(x_new, *tail_in, *refs[:len(IN_PROJ_OUTS)])


def _stage(head_args, tail_args, *, first, last, seq, tm=STAGE_TM):
    x = head_args[0] if first else head_args[2]
    t = x.shape[0]
    row = lambda i: (i, 0)
    n_pos_blocks = seq // tm
    pos = lambda i: (i % n_pos_blocks, 0)
    if first:
        in_specs = [pl.BlockSpec((tm, D_MODEL), row)]
    else:
        in_specs = [pl.BlockSpec((tm, A_WIDTH), row), pl.BlockSpec((tm, MLA_WIDTH), row),
                    pl.BlockSpec((tm, D_MODEL), row)] + [_const_spec(c.shape) for c in head_args[3:]]
    out_specs = [pl.BlockSpec((tm, D_MODEL), row)]
    out_shape = [jax.ShapeDtypeStruct((t, D_MODEL), F32)]
    if not last:
        in_specs += [_const_spec(c.shape) for c in tail_args[:5]]
        in_specs += [pl.BlockSpec((tm, LANES), pos), pl.BlockSpec((tm, LANES), pos)]
        out_specs += [pl.BlockSpec((tm, w), row) for w, _ in IN_PROJ_OUTS]
        out_shape += [jax.ShapeDtypeStruct((t, w), dt) for w, dt in IN_PROJ_OUTS]
    scratch = [pltpu.VMEM((D_MODEL // LANES, tm, LANES), F32)] if (first or last) else []
    return pl.pallas_call(
        functools.partial(_stage_kernel, first=first, last=last),
        grid=(t // tm,),
        in_specs=in_specs, out_specs=out_specs, out_shape=out_shape, scratch_shapes=scratch,
        compiler_params=_params(1, STAGE_VMEM_LIMIT),
        name="stage_first" if first else ("stage_last" if last else "stage"),
    )(*head_args, *tail_args)


Q_TILES = 4


def _mla_kernel(q_ref, k_ref, v_ref, o_ref, m_sc, acc_sc, *, tq):
    first_diag = Q_TILES * pl.program_id(1)
    tk = tq
    m_sc[...] = jnp.full(m_sc.shape, NEG_INF, F32)
    acc_sc[...] = jnp.zeros(acc_sc.shape, F32)

    def tile(qt, k_tile, causal):
        k_start = pl.multiple_of(k_tile * tk, tk)
        for h in range(MLA_HEADS):
            q = q_ref[qt * tq:(qt + 1) * tq, h * MLA_KPAD:(h + 1) * MLA_KPAD]
            k = k_ref[pl.ds(k_start, tk), h * MLA_KPAD:(h + 1) * MLA_KPAD]
            v = v_ref[pl.ds(k_start, tk), h * V_DIM:(h + 1) * V_DIM]
            s = lax.dot_general(q, k, (((1,), (1,)), ((), ())), preferred_element_type=F32)
            if causal is not None:
                s = jnp.where(causal, s, NEG_INF)
            m_prev = m_sc[qt, h]
            m_new = jnp.maximum(m_prev, jnp.max(s, -1, keepdims=True))
            alpha = jnp.exp2(m_prev - m_new)
            p = jnp.exp2(s - jnp.concatenate([m_new] * (tk // LANES), axis=1))
            v1 = jnp.concatenate([v, jnp.ones_like(v)], axis=1)
            acc_sc[qt, h] = (jnp.concatenate([alpha, alpha], axis=1) * acc_sc[qt, h]
                             + jnp.dot(p.astype(BF16), v1, preferred_element_type=F32))
            m_sc[qt, h] = m_new

    def below_diagonal(k_tile, carry):
        for qt in range(Q_TILES):
            tile(qt, k_tile, None)
        return carry

    lax.fori_loop(0, first_diag, below_diagonal, 0)

    r = lax.broadcasted_iota(jnp.int32, (tq, tk), 0)
    c = lax.broadcasted_iota(jnp.int32, (tq, tk), 1)
    causal = ((c & -TOKEN_GROUP) + _position_in_group(c)) <= ((r & -TOKEN_GROUP) + _position_in_group(r))
    for qt in range(Q_TILES):
        for kt in range(qt + 1):
            tile(qt, first_diag + kt, causal if kt == qt else None)
    for qt in range(Q_TILES):
        for h in range(MLA_HEADS):
            acc = acc_sc[qt, h]
            o_ref[qt * tq:(qt + 1) * tq, h * V_DIM:(h + 1) * V_DIM] = acc[:, :V_DIM] / acc[:, V_DIM:]


def _mla_attention(qm, km, vm, *, batch, seq, tq):
    t = qm.shape[0]
    steps = seq // (Q_TILES * tq)
    q_map = lambda b, i: (b * steps + i, 0)
    kv_map = lambda b, i: (b, 0)
    return pl.pallas_call(
        functools.partial(_mla_kernel, tq=tq),
        grid=(batch, steps),
        in_specs=[
            pl.BlockSpec((Q_TILES * tq, MLA_HEADS * MLA_KPAD), q_map),
            pl.BlockSpec((seq, MLA_HEADS * MLA_KPAD), kv_map),
            pl.BlockSpec((seq, MLA_WIDTH), kv_map),
        ],
        out_specs=pl.BlockSpec((Q_TILES * tq, MLA_WIDTH), q_map),
        out_shape=jax.ShapeDtypeStruct((t, MLA_WIDTH), F32),
        scratch_shapes=[pltpu.VMEM((Q_TILES, MLA_HEADS, tq, LANES), F32),
                        pltpu.VMEM((Q_TILES, MLA_HEADS, tq, 2 * V_DIM), F32)],
        compiler_params=_params(2),
        name="mla_attention",
    )(qm, km, vm)


HEAD_PAIRS = A_WIDTH // LANES


def _sub_index(i, dilation):
    if dilation == 16:
        return i
    if dilation == 4:
        return 32 * (i >> 5) + 4 * (i & 7) + ((i >> 3) & 3)
    return _position_in_group(i)


def _band_bias(dilation, has_prev):
    i = lax.broadcasted_iota(jnp.int32, (2 * SPAN, 2 * SPAN), 0) & (SPAN - 1)
    j = lax.broadcasted_iota(jnp.int32, (2 * SPAN, 2 * SPAN), 1)
    dist = SPAN + _sub_index(i, dilation) - ((j & SPAN) + _sub_index(j & (SPAN - 1), dilation))
    valid = (dist >= 0) & (dist <= SPAN)
    if has_prev is not None:
        valid = valid & ((j >= SPAN) | has_prev)
    return jnp.where(valid, 0.0, NEG_INF)


def _dilated_kernel(q_ref, k_ref, v_ref, o_ref, kp_sc, vp_sc, m_sc, l_sc):
    @pl.when((pl.program_id(0) == 0) & (pl.program_id(1) == 0))
    def _():
        kp_sc[...] = jnp.zeros(kp_sc.shape, F32)
        vp_sc[...] = jnp.zeros(vp_sc.shape, F32)

    has_prev = pl.program_id(1) > 0
    lane = lax.broadcasted_iota(jnp.int32, (SPAN, LANES), 1)
    head_a = lane < A_HEAD_DIM
    ones = jnp.ones((2 * SPAN, LANES), BF16)

    def rows2d(t):
        return t.reshape(-1, LANES)

    def attend(q, k, v, bias, idx, first, last):
        qa = jnp.where(head_a, q, 0.0).astype(BF16)
        qb = jnp.where(head_a, 0.0, q).astype(BF16)
        q2 = jnp.concatenate([qa, qb], axis=0)
        s = lax.dot_general(q2, k.astype(BF16), (((1,), (1,)), ((), ())), preferred_element_type=F32) + bias
        m = jnp.max(s, -1, keepdims=True)
        p = jnp.exp2(s - m)
        pv = jnp.dot(p.astype(BF16), jnp.concatenate([v.astype(BF16), ones], axis=1),
                     preferred_element_type=F32)
        acc = jnp.where(head_a, pv[:SPAN, :LANES], pv[SPAN:, :LANES])
        l = jnp.where(head_a, pv[:SPAN, LANES:], pv[SPAN:, LANES:])
        m_cur = jnp.where(head_a, jnp.broadcast_to(m[:SPAN], (SPAN, LANES)),
                          jnp.broadcast_to(m[SPAN:], (SPAN, LANES)))
        shape = o_ref[idx].shape
        if first:
            o_ref[idx] = acc.reshape(shape)
            l_sc[idx] = l.reshape(shape)
            m_sc[idx] = m_cur.reshape(shape)
            return
        m_old = rows2d(m_sc[idx])
        m_new = jnp.maximum(m_old, m_cur)
        a_old = jnp.exp2(m_old - m_new)
        a_cur = jnp.exp2(m_cur - m_new)
        acc = a_old * rows2d(o_ref[idx]) + a_cur * acc
        l = a_old * rows2d(l_sc[idx]) + a_cur * l
        if last:
            o_ref[idx] = (acc / l).reshape(shape)
        else:
            o_ref[idx] = acc.reshape(shape)
            l_sc[idx] = l.reshape(shape)
            m_sc[idx] = m_new.reshape(shape)

    every = slice(None)

    band = _band_bias(1, None)
    band_first = _band_bias(1, has_prev)
    for p in range(HEAD_PAIRS):
        cols = pl.ds(p * LANES, LANES)
        k = jnp.concatenate([rows2d(kp_sc[CHUNK_GROUPS - 1, :, :, cols]), rows2d(k_ref[0, :, :, cols])], axis=0)
        v = jnp.concatenate([rows2d(vp_sc[CHUNK_GROUPS - 1, :, :, cols]), rows2d(v_ref[0, :, :, cols])], axis=0)
        attend(rows2d(q_ref[0, :, :, cols]), k, v, band_first, (0, every, every, cols), True, False)

    def d1_body(g, carry):
        for p in range(HEAD_PAIRS):
            cols = pl.ds(p * LANES, LANES)
            attend(rows2d(q_ref[g, :, :, cols]), rows2d(k_ref[pl.ds(g - 1, 2), :, :, cols]),
                   rows2d(v_ref[pl.ds(g - 1, 2), :, :, cols]), band, (g, every, every, cols), True, False)
        return carry
    lax.fori_loop(1, CHUNK_GROUPS, d1_body, 0, unroll=True)

    band = _band_bias(4, None)
    band_first = _band_bias(4, has_prev)

    def d4_first(r4, carry):
        res = pl.ds(4 * r4, 4)
        for p in range(HEAD_PAIRS):
            cols = pl.ds(p * LANES, LANES)
            k = jnp.concatenate([rows2d(kp_sc[CHUNK_GROUPS - 4:, res, :, cols]), rows2d(k_ref[0:4, res, :, cols])],
                                axis=0)
            v = jnp.concatenate([rows2d(vp_sc[CHUNK_GROUPS - 4:, res, :, cols]), rows2d(v_ref[0:4, res, :, cols])],
                                axis=0)
            attend(rows2d(q_ref[0:4, res, :, cols]), k, v, band_first, (slice(0, 4), res, every, cols), False, False)
        return carry
    lax.fori_loop(0, 4, d4_first, 0, unroll=True)

    blocks_after_first = CHUNK_GROUPS // 4 - 1

    def d4_body(t, carry):
        res = pl.ds(4 * (t // blocks_after_first), 4)
        g0 = 4 * (1 + t % blocks_after_first)
        for p in range(HEAD_PAIRS):
            cols = pl.ds(p * LANES, LANES)
            attend(rows2d(q_ref[pl.ds(g0, 4), res, :, cols]), rows2d(k_ref[pl.ds(g0 - 4, 8), res, :, cols]),
                   rows2d(v_ref[pl.ds(g0 - 4, 8), res, :, cols]), band, (pl.ds(g0, 4), res, every, cols),
                   False, False)
        return carry
    lax.fori_loop(0, 4 * blocks_after_first, d4_body, 0, unroll=True)

    band16 = _band_bias(16, has_prev)

    def d16_body(r, carry):
        for p in range(HEAD_PAIRS):
            cols = pl.ds(p * LANES, LANES)
            k = jnp.concatenate([rows2d(kp_sc[:, r, :, cols]), rows2d(k_ref[:, r, :, cols])], axis=0)
            v = jnp.concatenate([rows2d(vp_sc[:, r, :, cols]), rows2d(v_ref[:, r, :, cols])], axis=0)
            attend(rows2d(q_ref[:, r, :, cols]), k, v, band16, (every, r, every, cols), False, True)
        return carry
    lax.fori_loop(0, MAX_DILATION, d16_body, 0, unroll=True)

    kp_sc[...] = k_ref[...]
    vp_sc[...] = v_ref[...]


def _dilated_attention(qa, ka, va, *, batch, seq):
    t = qa.shape[0]
    chunks = seq // (CHUNK_GROUPS * TOKEN_GROUP)
    tiles = TOKEN_GROUP // SUBLANES
    shape4 = (t // TOKEN_GROUP, tiles, SUBLANES, A_WIDTH)
    view = lambda a: a.reshape(shape4)
    block = (CHUNK_GROUPS, tiles, SUBLANES, A_WIDTH)
    spec = pl.BlockSpec(block, lambda b, c: (b * chunks + c, 0, 0, 0))
    scratch = pltpu.VMEM(block, F32)
    out = pl.pallas_call(
        _dilated_kernel,
        grid=(batch, chunks),
        in_specs=[spec, spec, spec],
        out_specs=spec,
        out_shape=jax.ShapeDtypeStruct(shape4, F32),
        scratch_shapes=[scratch, scratch, scratch, scratch],
        compiler_params=_params(2),
        name="dilated",
    )(view(qa), view(ka), view(va))
    return out.reshape(t, A_WIDTH)


def _rope_tables(seq):
    half = A_HEAD_DIM // 2
    inv_freq = ROPE_THETA ** (-jnp.arange(half, dtype=F32) / half)
    row = jnp.arange(seq, dtype=jnp.int32)
    pos = (row & -TOKEN_GROUP) + _position_in_group(row)
    ang = pos.astype(F32)[:, None] * inv_freq[None, :]
    cos = jnp.cos(ang)
    sin = jnp.sin(ang)
    cos128 = jnp.tile(cos, (1, LANES // half))
    sin128 = jnp.tile(jnp.concatenate([-sin, sin], axis=1), (1, LANES // A_HEAD_DIM))
    return cos128, sin128


def _prep_w_uq(w):
    w = w.reshape(Q_LORA, MLA_HEADS, QK_NOPE + QK_ROPE)
    w = jnp.pad(w, ((0, 0), (0, 0), (0, MLA_KPAD - QK_NOPE - QK_ROPE)))
    return w.reshape(Q_LORA, MLA_HEADS * MLA_KPAD).astype(BF16)


def _prep_w_ukv(w):
    w = w.reshape(KV_LORA, MLA_HEADS, QK_NOPE + V_DIM)
    k = w[:, :, :QK_NOPE].reshape(KV_LORA, MLA_HEADS * QK_NOPE)
    v = w[:, :, QK_NOPE:].reshape(KV_LORA, MLA_HEADS * V_DIM)
    return jnp.concatenate([k, v], axis=1).astype(BF16)


def kernel(x, w_in, q_a_norm, kv_a_norm, w_uq, w_ukv, a_out_norm, b_out_norm, w_o,
           ln1_g, ln1_b, w_ff1, w_ff2, ln2_g, ln2_b):
    batch, seq, _ = x.shape
    assert seq % (CHUNK_GROUPS * TOKEN_GROUP) == 0
    t = batch * seq
    cos, sin = _rope_tables(seq)
    row2d = lambda v: v.reshape(1, -1)

    def in_proj_args(l):
        w_in_l = jnp.pad(w_in[l], ((0, 0), (0, IN_COLS_PAD - w_in.shape[2]))).astype(BF16)
        return (w_in_l, _prep_w_uq(w_uq[l]), _prep_w_ukv(w_ukv[l]), row2d(q_a_norm[l]), row2d(kv_a_norm[l]),
                cos, sin)

    def mix_args(l):
        return (w_o[l].astype(BF16), row2d(a_out_norm[l]), row2d(b_out_norm[l]), row2d(ln1_g[l]),
                row2d(ln1_b[l]), w_ff1[l].astype(BF16), w_ff2[l].astype(BF16), row2d(ln2_g[l]), row2d(ln2_b[l]))

    xf, qa, ka, va, qm, km, vm = _stage((x.reshape(t, D_MODEL),), in_proj_args(0),
                                        first=True, last=False, seq=seq)
    for l in range(DEPTH):
        b_out = _mla_attention(qm, km, vm, batch=batch, seq=seq, tq=512)
        a_out = _dilated_attention(qa, ka, va, batch=batch, seq=seq)
        head = (a_out, b_out, xf) + mix_args(l)
        if l + 1 < DEPTH:
            xf, qa, ka, va, qm, km, vm = _stage(head, in_proj_args(l + 1), first=False, last=False, seq=seq)
        else:
            (xf,) = _stage(head, (), first=False, last=True, seq=seq)
    return xf.reshape(batch, seq, D_MODEL)
```

```python
import functools

import jax
import jax.numpy as jnp
from jax import lax
from jax.experimental import pallas as pl
from jax.experimental.pallas import tpu as pltpu

D_MODEL = 1024
DEPTH = 4
A_HEADS = 8
A_HEAD_DIM = 64
A_WIDTH = A_HEADS * A_HEAD_DIM
MLA_HEADS = 4
QK_NOPE = 128
QK_ROPE = 64
V_DIM = 128
Q_LORA = 256
KV_LORA = 128
MLA_WIDTH = MLA_HEADS * V_DIM
D_FF = 4 * D_MODEL
ROPE_THETA = 10000.0
ALPHA = (2.0 * DEPTH) ** 0.25
LN_EPS = 1e-5
RMS_EPS = 1e-6

LANES = 128
SUBLANES = 8
MLA_KPAD = 256
IN_COLS_PAD = 3 * A_WIDTH + Q_LORA + KV_LORA + LANES
VMEM_LIMIT = 56 * 1024 * 1024
STAGE_VMEM_LIMIT = 60 * 1024 * 1024
STAGE_TM = 512

SPAN = 128
MAX_DILATION = 16
TOKEN_GROUP = SPAN
CHUNK_GROUPS = SPAN * MAX_DILATION // TOKEN_GROUP

F32 = jnp.float32
BF16 = jnp.bfloat16
NEG_INF = float("-inf")
LOG2E = 1.4426950408889634
MLA_Q_SCALE = (QK_NOPE + QK_ROPE) ** -0.5 * LOG2E
A_Q_SCALE = A_HEAD_DIM ** -0.5 * LOG2E


def _const_spec(shape):
    return pl.BlockSpec(shape, lambda *_: (0,) * len(shape), pipeline_mode=pl.Buffered(1))


def _params(n_axes, vmem_limit=VMEM_LIMIT):
    return pltpu.CompilerParams(dimension_semantics=("arbitrary",) * n_axes,
                                vmem_limit_bytes=vmem_limit)


def _rms(x, g):
    return x * lax.rsqrt(jnp.mean(x * x, -1, keepdims=True) + RMS_EPS) * g


def _layer_norm(z, g, b):
    mu = jnp.mean(z, -1, keepdims=True)
    zc = z - mu
    var = jnp.mean(zc * zc, -1, keepdims=True)
    return zc * lax.rsqrt(var + LN_EPS) * g + b


def _rope128(x, cos, sin_signed, first_half):
    rot = jnp.where(first_half, pltpu.roll(x, 96, 1), pltpu.roll(x, 32, 1))
    return x * cos + rot * sin_signed


def _position_in_group(i):
    return 16 * (i & 7) + 4 * ((i >> 3) & 3) + ((i >> 5) & 3)


FF_CHUNK = 1024
IN_PROJ_OUTS = ((A_WIDTH, F32), (A_WIDTH, F32), (A_WIDTH, F32),
                (MLA_HEADS * MLA_KPAD, BF16), (MLA_HEADS * MLA_KPAD, BF16), (MLA_WIDTH, BF16))


def _in_proj_tail(x, w_in_ref, w_uq_ref, w_ukv_ref, gq_ref, gkv_ref, cos_ref, sin_ref,
                  qa_ref, ka_ref, va_ref, qm_ref, km_ref, vm_ref):
    h = jnp.dot(x.astype(BF16), w_in_ref[...], preferred_element_type=F32)
    cos = cos_ref[...]
    sin = sin_ref[...]
    lane = lax.broadcasted_iota(jnp.int32, cos.shape, 1)
    first_half = (lane & (A_HEAD_DIM - 1)) < (A_HEAD_DIM // 2)

    def rope(t):
        return _rope128(t, cos, sin, first_half)

    def rope_cols(t):
        return jnp.concatenate(
            [rope(t[:, c * LANES:(c + 1) * LANES]) for c in range(t.shape[1] // LANES)], axis=1)

    qa_ref[...] = rope_cols(h[:, 0:A_WIDTH]) * A_Q_SCALE
    ka_ref[...] = rope_cols(h[:, A_WIDTH:2 * A_WIDTH])
    va_ref[...] = h[:, 2 * A_WIDTH:3 * A_WIDTH]

    c0 = 3 * A_WIDTH
    cq = _rms(h[:, c0:c0 + Q_LORA], gq_ref[...])
    q = jnp.dot(cq.astype(BF16), w_uq_ref[...], preferred_element_type=F32)
    q_parts = []
    for hh in range(MLA_HEADS):
        base = hh * MLA_KPAD
        q_parts.append(q[:, base:base + QK_NOPE])
        q_parts.append(rope(q[:, base + QK_NOPE:base + MLA_KPAD]))
    qm_ref[...] = (jnp.concatenate(q_parts, axis=1) * MLA_Q_SCALE).astype(BF16)

    c1 = c0 + Q_LORA
    ckv = _rms(h[:, c1:c1 + KV_LORA], gkv_ref[...])
    kv = jnp.dot(ckv.astype(BF16), w_ukv_ref[...], preferred_element_type=F32)
    c2 = c1 + KV_LORA
    k_pe = rope(h[:, c2:c2 + LANES])
    k_parts = []
    for hh in range(MLA_HEADS):
        k_parts.append(kv[:, hh * QK_NOPE:(hh + 1) * QK_NOPE])
        k_parts.append(k_pe)
    km_ref[...] = jnp.concatenate(k_parts, axis=1).astype(BF16)
    vm_ref[...] = kv[:, MLA_HEADS * QK_NOPE:].astype(BF16)


def _mix_ffn_head(a_ref, b_ref, x_ref, wo_ref, ga_ref, gb_ref, g1_ref, beta1_ref,
                  w1_ref, w2_ref, g2_ref, beta2_ref):
    mixed = jnp.concatenate([_rms(a_ref[...], ga_ref[...]), _rms(b_ref[...], gb_ref[...])], axis=1)
    y = jnp.dot(mixed.astype(BF16), wo_ref[...], preferred_element_type=F32)
    x1 = _layer_norm(ALPHA * x_ref[...] + y, g1_ref[...], beta1_ref[...])
    xb = x1.astype(BF16)
    acc = ALPHA * x1
    for c in range(D_FF // FF_CHUNK):
        cols = slice(c * FF_CHUNK, (c + 1) * FF_CHUNK)
        hdn = jnp.dot(xb, w1_ref[:, cols], preferred_element_type=F32)
        hdn = jnp.square(jnp.maximum(hdn, 0.0)).astype(BF16)
        acc = acc + jnp.dot(hdn, w2_ref[cols, :], preferred_element_type=F32)
    return _layer_norm(acc, g2_ref[...], beta2_ref[...])


def _permute_rows(src, o_ref, slab_sc, *, inverse):
    tm, width = o_ref.shape
    tiles = TOKEN_GROUP // SUBLANES
    if not inverse:
        for c in range(width // LANES):
            slab_sc[c] = src[:, c * LANES:(c + 1) * LANES]
    for g in range(tm // TOKEN_GROUP):
        for tile in range(tiles):
            first_token = 4 * (tile & 3) + (tile >> 2)
            natural = pl.ds(g * TOKEN_GROUP + first_token, SUBLANES, stride=tiles)
            p0 = g * TOKEN_GROUP + tile * SUBLANES
            for c in range(width // LANES):
                if inverse:
                    slab_sc[c, natural, :] = src[p0:p0 + SUBLANES, c * LANES:(c + 1) * LANES]
                else:
                    o_ref[p0:p0 + SUBLANES, c * LANES:(c + 1) * LANES] = slab_sc[c, natural, :]
    if inverse:
        for c in range(width // LANES):
            o_ref[:, c * LANES:(c + 1) * LANES] = slab_sc[c]


N_MIX_INPUTS = 12
N_IN_PROJ_INPUTS = 7


def _stage_kernel(*refs, first, last):
    refs = list(refs)
    if first:
        head_refs, refs = refs[:1], refs[1:]
    else:
        head_refs, refs = refs[:N_MIX_INPUTS], refs[N_MIX_INPUTS:]
    if not last:
        tail_in, refs = refs[:N_IN_PROJ_INPUTS], refs[N_IN_PROJ_INPUTS:]
    x_out_ref, refs = refs[0], refs[1:]
    if first:
        _permute_rows(head_refs[0], x_out_ref, refs[-1], inverse=False)
        x_new = x_out_ref[...]
    else:
        x_new = _mix_ffn_head(*head_refs)
        if last:
            _permute_rows(x_new, x_out_ref, refs[-1], inverse=True)
        else:
            x_out_ref[...] = x_new
    if not last:
        _in_proj_tail(x_new, *tail_in, *refs[:len(IN_PROJ_OUTS)])


def _stage(head_args, tail_args, *, first, last, seq, tm=STAGE_TM):
    x = head_args[0] if first else head_args[2]
    t = x.shape[0]
    row = lambda i: (i, 0)
    n_pos_blocks = seq // tm
    pos = lambda i: (i % n_pos_blocks, 0)
    if first:
        in_specs = [pl.BlockSpec((tm, D_MODEL), row)]
    else:
        in_specs = [pl.BlockSpec((tm, A_WIDTH), row), pl.BlockSpec((tm, MLA_WIDTH), row),
                    pl.BlockSpec((tm, D_MODEL), row)] + [_const_spec(c.shape) for c in head_args[3:]]
    out_specs = [pl.BlockSpec((tm, D_MODEL), row)]
    out_shape = [jax.ShapeDtypeStruct((t, D_MODEL), F32)]
    if not last:
        in_specs += [_const_spec(c.shape) for c in tail_args[:5]]
        in_specs += [pl.BlockSpec((tm, LANES), pos), pl.BlockSpec((tm, LANES), pos)]
        out_specs += [pl.BlockSpec((tm, w), row) for w, _ in IN_PROJ_OUTS]
        out_shape += [jax.ShapeDtypeStruct((t, w), dt) for w, dt in IN_PROJ_OUTS]
    scratch = [pltpu.VMEM((D_MODEL // LANES, tm, LANES), F32)] if (first or last) else []
    return pl.pallas_call(
        functools.partial(_stage_kernel, first=first, last=last),
        grid=(t // tm,),
        in_specs=in_specs, out_specs=out_specs, out_shape=out_shape, scratch_shapes=scratch,
        compiler_params=_params(1, STAGE_VMEM_LIMIT),
        name="stage_first" if first else ("stage_last" if last else "stage"),
    )(*head_args, *tail_args)


Q_TILES = 4


def _mla_kernel(q_ref, k_ref, v_ref, o_ref, m_sc, acc_sc, *, tq):
    first_diag = Q_TILES * pl.program_id(1)
    tk = tq
    m_sc[...] = jnp.full(m_sc.shape, NEG_INF, F32)
    acc_sc[...] = jnp.zeros(acc_sc.shape, F32)

    def tile(qt, k_tile, causal, q_lo=0, k_lo=0, k_n=tk):
        k_start = pl.multiple_of(k_tile * tk, tk) + k_lo
        rows = slice(q_lo, tq)
        for h in range(MLA_HEADS):
            q = q_ref[qt * tq + q_lo:(qt + 1) * tq, h * MLA_KPAD:(h + 1) * MLA_KPAD]
            k = k_ref[pl.ds(k_start, k_n), h * MLA_KPAD:(h + 1) * MLA_KPAD]
            v = v_ref[pl.ds(k_start, k_n), h * V_DIM:(h + 1) * V_DIM]
            s = lax.dot_general(q, k, (((1,), (1,)), ((), ())), preferred_element_type=F32)
            if causal is not None:
                s = jnp.where(causal, s, NEG_INF)
            m_prev = m_sc[qt, h, rows]
            m_new = jnp.maximum(m_prev, jnp.max(s, -1, keepdims=True))
            alpha = jnp.exp2(m_prev - m_new)
            p = jnp.exp2(s - jnp.concatenate([m_new] * (k_n // LANES), axis=1))
            v1 = jnp.concatenate([v, jnp.ones_like(v)], axis=1)
            acc_sc[qt, h, rows] = (jnp.concatenate([alpha, alpha], axis=1) * acc_sc[qt, h, rows]
                                   + jnp.dot(p.astype(BF16), v1, preferred_element_type=F32))
            m_sc[qt, h, rows] = m_new

    def below_diagonal(k_tile, carry):
        for qt in range(Q_TILES):
            tile(qt, k_tile, None)
        return carry

    lax.fori_loop(0, first_diag, below_diagonal, 0)

    r = lax.broadcasted_iota(jnp.int32, (tq, tk), 0)
    c = lax.broadcasted_iota(jnp.int32, (tq, tk), 1)
    causal = ((c & -TOKEN_GROUP) + _position_in_group(c)) <= ((r & -TOKEN_GROUP) + _position_in_group(r))
    half = tq // 2
    for qt in range(Q_TILES):
        for kt in range(qt):
            tile(qt, first_diag + kt, None)
        tile(qt, first_diag + qt, causal[:, :half], k_n=half)
        tile(qt, first_diag + qt, causal[half:, half:], q_lo=half, k_lo=half, k_n=half)
    for qt in range(Q_TILES):
        for h in range(MLA_HEADS):
            acc = acc_sc[qt, h]
            o_ref[qt * tq:(qt + 1) * tq, h * V_DIM:(h + 1) * V_DIM] = acc[:, :V_DIM] / acc[:, V_DIM:]


def _mla_attention(qm, km, vm, *, batch, seq, tq):
    t = qm.shape[0]
    steps = seq // (Q_TILES * tq)
    q_map = lambda b, i: (b * steps + i, 0)
    kv_map = lambda b, i: (b, 0)
    return pl.pallas_call(
        functools.partial(_mla_kernel, tq=tq),
        grid=(batch, steps),
        in_specs=[
            pl.BlockSpec((Q_TILES * tq, MLA_HEADS * MLA_KPAD), q_map),
            pl.BlockSpec((seq, MLA_HEADS * MLA_KPAD), kv_map),
            pl.BlockSpec((seq, MLA_WIDTH), kv_map),
        ],
        out_specs=pl.BlockSpec((Q_TILES * tq, MLA_WIDTH), q_map),
        out_shape=jax.ShapeDtypeStruct((t, MLA_WIDTH), F32),
        scratch_shapes=[pltpu.VMEM((Q_TILES, MLA_HEADS, tq, LANES), F32),
                        pltpu.VMEM((Q_TILES, MLA_HEADS, tq, 2 * V_DIM), F32)],
        compiler_params=_params(2),
        name="mla_attention",
    )(qm, km, vm)


HEAD_PAIRS = A_WIDTH // LANES


def _sub_index(i, dilation):
    if dilation == 16:
        return i
    if dilation == 4:
        return 32 * (i >> 5) + 4 * (i & 7) + ((i >> 3) & 3)
    return _position_in_group(i)


def _band_bias(dilation, has_prev):
    i = lax.broadcasted_iota(jnp.int32, (2 * SPAN, 2 * SPAN), 0) & (SPAN - 1)
    j = lax.broadcasted_iota(jnp.int32, (2 * SPAN, 2 * SPAN), 1)
    dist = SPAN + _sub_index(i, dilation) - ((j & SPAN) + _sub_index(j & (SPAN - 1), dilation))
    valid = (dist >= 0) & (dist <= SPAN)
    if has_prev is not None:
        valid = valid & ((j >= SPAN) | has_prev)
    return jnp.where(valid, 0.0, NEG_INF)


def _dilated_kernel(q_ref, k_ref, v_ref, o_ref, kp_sc, vp_sc, m_sc, l_sc):
    @pl.when((pl.program_id(0) == 0) & (pl.program_id(1) == 0))
    def _():
        kp_sc[...] = jnp.zeros(kp_sc.shape, F32)
        vp_sc[...] = jnp.zeros(vp_sc.shape, F32)

    has_prev = pl.program_id(1) > 0
    lane = lax.broadcasted_iota(jnp.int32, (SPAN, LANES), 1)
    head_a = lane < A_HEAD_DIM
    ones = jnp.ones((2 * SPAN, LANES), BF16)

    def rows2d(t):
        return t.reshape(-1, LANES)

    def attend(q, k, v, bias, idx, first, last):
        qa = jnp.where(head_a, q, 0.0).astype(BF16)
        qb = jnp.where(head_a, 0.0, q).astype(BF16)
        q2 = jnp.concatenate([qa, qb], axis=0)
        s = lax.dot_general(q2, k.astype(BF16), (((1,), (1,)), ((), ())), preferred_element_type=F32) + bias
        m = jnp.max(s, -1, keepdims=True)
        p = jnp.exp2(s - m)
        pv = jnp.dot(p.astype(BF16), jnp.concatenate([v.astype(BF16), ones], axis=1),
                     preferred_element_type=F32)
        acc = jnp.where(head_a, pv[:SPAN, :LANES], pv[SPAN:, :LANES])
        l = jnp.where(head_a, pv[:SPAN, LANES:], pv[SPAN:, LANES:])
        m_cur = jnp.where(head_a, jnp.broadcast_to(m[:SPAN], (SPAN, LANES)),
                          jnp.broadcast_to(m[SPAN:], (SPAN, LANES)))
        shape = o_ref[idx].shape
        if first:
            o_ref[idx] = acc.reshape(shape)
            l_sc[idx] = l.reshape(shape)
            m_sc[idx] = m_cur.reshape(shape)
            return
        m_old = rows2d(m_sc[idx])
        m_new = jnp.maximum(m_old, m_cur)
        a_old = jnp.exp2(m_old - m_new)
        a_cur = jnp.exp2(m_cur - m_new)
        acc = a_old * rows2d(o_ref[idx]) + a_cur * acc
        l = a_old * rows2d(l_sc[idx]) + a_cur * l
        if last:
            o_ref[idx] = (acc / l).reshape(shape)
        else:
            o_ref[idx] = acc.reshape(shape)
            l_sc[idx] = l.reshape(shape)
            m_sc[idx] = m_new.reshape(shape)

    every = slice(None)

    band = _band_bias(1, None)
    band_first = _band_bias(1, has_prev)
    for p in range(HEAD_PAIRS):
        cols = pl.ds(p * LANES, LANES)
        k = jnp.concatenate([rows2d(kp_sc[CHUNK_GROUPS - 1, :, :, cols]), rows2d(k_ref[0, :, :, cols])], axis=0)
        v = jnp.concatenate([rows2d(vp_sc[CHUNK_GROUPS - 1, :, :, cols]), rows2d(v_ref[0, :, :, cols])], axis=0)
        attend(rows2d(q_ref[0, :, :, cols]), k, v, band_first, (0, every, every, cols), True, False)

    def d1_body(g, carry):
        for p in range(HEAD_PAIRS):
            cols = pl.ds(p * LANES, LANES)
            attend(rows2d(q_ref[g, :, :, cols]), rows2d(k_ref[pl.ds(g - 1, 2), :, :, cols]),
                   rows2d(v_ref[pl.ds(g - 1, 2), :, :, cols]), band, (g, every, every, cols), True, False)
        return carry
    lax.fori_loop(1, CHUNK_GROUPS, d1_body, 0, unroll=True)

    band = _band_bias(4, None)
    band_first = _band_bias(4, has_prev)

    def d4_first(r4, carry):
        res = pl.ds(4 * r4, 4)
        for p in range(HEAD_PAIRS):
            cols = pl.ds(p * LANES, LANES)
            k = jnp.concatenate([rows2d(kp_sc[CHUNK_GROUPS - 4:, res, :, cols]), rows2d(k_ref[0:4, res, :, cols])],
                                axis=0)
            v = jnp.concatenate([rows2d(vp_sc[CHUNK_GROUPS - 4:, res, :, cols]), rows2d(v_ref[0:4, res, :, cols])],
                                axis=0)
            attend(rows2d(q_ref[0:4, res, :, cols]), k, v, band_first, (slice(0, 4), res, every, cols), False, False)
        return carry
    lax.fori_loop(0, 4, d4_first, 0, unroll=True)

    blocks_after_first = CHUNK_GROUPS // 4 - 1

    def d4_body(t, carry):
        res = pl.ds(4 * (t // blocks_after_first), 4)
        g0 = 4 * (1 + t % blocks_after_first)
        for p in range(HEAD_PAIRS):
            cols = pl.ds(p * LANES, LANES)
            attend(rows2d(q_ref[pl.ds(g0, 4), res, :, cols]), rows2d(k_ref[pl.ds(g0 - 4, 8), res, :, cols]),
                   rows2d(v_ref[pl.ds(g0 - 4, 8), res, :, cols]), band, (pl.ds(g0, 4), res, every, cols),
                   False, False)
        return carry
    lax.fori_loop(0, 4 * blocks_after_first, d4_body, 0, unroll=True)

    band16 = _band_bias(16, has_prev)

    def d16_body(r, carry):
        for p in range(HEAD_PAIRS):
            cols = pl.ds(p * LANES, LANES)
            k = jnp.concatenate([rows2d(kp_sc[:, r, :, cols]), rows2d(k_ref[:, r, :, cols])], axis=0)
            v = jnp.concatenate([rows2d(vp_sc[:, r, :, cols]), rows2d(v_ref[:, r, :, cols])], axis=0)
            attend(rows2d(q_ref[:, r, :, cols]), k, v, band16, (every, r, every, cols), False, True)
        return carry
    lax.fori_loop(0, MAX_DILATION, d16_body, 0, unroll=True)

    kp_sc[...] = k_ref[...]
    vp_sc[...] = v_ref[...]


def _dilated_attention(qa, ka, va, *, batch, seq):
    t = qa.shape[0]
    chunks = seq // (CHUNK_GROUPS * TOKEN_GROUP)
    tiles = TOKEN_GROUP // SUBLANES
    shape4 = (t // TOKEN_GROUP, tiles, SUBLANES, A_WIDTH)
    view = lambda a: a.reshape(shape4)
    block = (CHUNK_GROUPS, tiles, SUBLANES, A_WIDTH)
    spec = pl.BlockSpec(block, lambda b, c: (b * chunks + c, 0, 0, 0))
    scratch = pltpu.VMEM(block, F32)
    out = pl.pallas_call(
        _dilated_kernel,
        grid=(batch, chunks),
        in_specs=[spec, spec, spec],
        out_specs=spec,
        out_shape=jax.ShapeDtypeStruct(shape4, F32),
        scratch_shapes=[scratch, scratch, scratch, scratch],
        compiler_params=_params(2),
        name="dilated",
    )(view(qa), view(ka), view(va))
    return out.reshape(t, A_WIDTH)


def _rope_tables(seq):
    half = A_HEAD_DIM // 2
    inv_freq = ROPE_THETA ** (-jnp.arange(half, dtype=F32) / half)
    row = jnp.arange(seq, dtype=jnp.int32)
    pos = (row & -TOKEN_GROUP) + _position_in_group(row)
    ang = pos.astype(F32)[:, None] * inv_freq[None, :]
    cos = jnp.cos(ang)
    sin = jnp.sin(ang)
    cos128 = jnp.tile(cos, (1, LANES // half))
    sin128 = jnp.tile(jnp.concatenate([-sin, sin], axis=1), (1, LANES // A_HEAD_DIM))
    return cos128, sin128


def _prep_w_uq(w):
    w = w.reshape(Q_LORA, MLA_HEADS, QK_NOPE + QK_ROPE)
    w = jnp.pad(w, ((0, 0), (0, 0), (0, MLA_KPAD - QK_NOPE - QK_ROPE)))
    return w.reshape(Q_LORA, MLA_HEADS * MLA_KPAD).astype(BF16)


def _prep_w_ukv(w):
    w = w.reshape(KV_LORA, MLA_HEADS, QK_NOPE + V_DIM)
    k = w[:, :, :QK_NOPE].reshape(KV_LORA, MLA_HEADS * QK_NOPE)
    v = w[:, :, QK_NOPE:].reshape(KV_LORA, MLA_HEADS * V_DIM)
    return jnp.concatenate([k, v], axis=1).astype(BF16)


def kernel(x, w_in, q_a_norm, kv_a_norm, w_uq, w_ukv, a_out_norm, b_out_norm, w_o,
           ln1_g, ln1_b, w_ff1, w_ff2, ln2_g, ln2_b):
    batch, seq, _ = x.shape
    assert seq % (CHUNK_GROUPS * TOKEN_GROUP) == 0
    t = batch * seq
    cos, sin = _rope_tables(seq)
    row2d = lambda v: v.reshape(1, -1)

    def in_proj_args(l):
        w_in_l = jnp.pad(w_in[l], ((0, 0), (0, IN_COLS_PAD - w_in.shape[2]))).astype(BF16)
        return (w_in_l, _prep_w_uq(w_uq[l]), _prep_w_ukv(w_ukv[l]), row2d(q_a_norm[l]), row2d(kv_a_norm[l]),
                cos, sin)

    def mix_args(l):
        return (w_o[l].astype(BF16), row2d(a_out_norm[l]), row2d(b_out_norm[l]), row2d(ln1_g[l]),
                row2d(ln1_b[l]), w_ff1[l].astype(BF16), w_ff2[l].astype(BF16), row2d(ln2_g[l]), row2d(ln2_b[l]))

    xf, qa, ka, va, qm, km, vm = _stage((x.reshape(t, D_MODEL),), in_proj_args(0),
                                        first=True, last=False, seq=seq)
    for l in range(DEPTH):
        b_out = _mla_attention(qm, km, vm, batch=batch, seq=seq, tq=512)
        a_out = _dilated_attention(qa, ka, va, batch=batch, seq=seq)
        head = (a_out, b_out, xf) + mix_args(l)
        if l + 1 < DEPTH:
            xf, qa, ka, va, qm, km, vm = _stage(head, in_proj_args(l + 1), first=False, last=False, seq=seq)
        else:
            (xf,) = _stage(head, (), first=False, last=True, seq=seq)
    return xf.reshape(batch, seq, D_MODEL)
```

```python
import functools

import jax
import jax.numpy as jnp
from jax import lax
from jax.experimental import pallas as pl
from jax.experimental.pallas import tpu as pltpu

D_MODEL = 1024
DEPTH = 4
A_HEADS = 8
A_HEAD_DIM = 64
A_WIDTH = A_HEADS * A_HEAD_DIM
MLA_HEADS = 4
QK_NOPE = 128
QK_ROPE = 64
V_DIM = 128
Q_LORA = 256
KV_LORA = 128
MLA_WIDTH = MLA_HEADS * V_DIM
D_FF = 4 * D_MODEL
ROPE_THETA = 10000.0
ALPHA = (2.0 * DEPTH) ** 0.25
LN_EPS = 1e-5
RMS_EPS = 1e-6

LANES = 128
SUBLANES = 8
MLA_KPAD = 256
IN_COLS_PAD = 3 * A_WIDTH + Q_LORA + KV_LORA + LANES
VMEM_LIMIT = 56 * 1024 * 1024
STAGE_VMEM_LIMIT = 60 * 1024 * 1024
STAGE_TM = 512

SPAN = 128
MAX_DILATION = 16
TOKEN_GROUP = SPAN
CHUNK_GROUPS = SPAN * MAX_DILATION // TOKEN_GROUP

F32 = jnp.float32
BF16 = jnp.bfloat16
NEG_INF = float("-inf")
LOG2E = 1.4426950408889634
MLA_Q_SCALE = (QK_NOPE + QK_ROPE) ** -0.5 * LOG2E
A_Q_SCALE = A_HEAD_DIM ** -0.5 * LOG2E


def _const_spec(shape):
    return pl.BlockSpec(shape, lambda *_: (0,) * len(shape), pipeline_mode=pl.Buffered(1))


def _params(n_axes, vmem_limit=VMEM_LIMIT):
    return pltpu.CompilerParams(dimension_semantics=("arbitrary",) * n_axes,
                                vmem_limit_bytes=vmem_limit)


def _rms(x, g):
    return x * lax.rsqrt(jnp.mean(x * x, -1, keepdims=True) + RMS_EPS) * g


def _layer_norm(z, g, b):
    mu = jnp.mean(z, -1, keepdims=True)
    zc = z - mu
    var = jnp.mean(zc * zc, -1, keepdims=True)
    return zc * lax.rsqrt(var + LN_EPS) * g + b


def _rope128(x, cos, sin_signed, first_half):
    rot = jnp.where(first_half, pltpu.roll(x, 96, 1), pltpu.roll(x, 32, 1))
    return x * cos + rot * sin_signed


def _position_in_group(i):
    return 16 * (i & 7) + 4 * ((i >> 3) & 3) + ((i >> 5) & 3)


FF_CHUNK = 1024
IN_PROJ_OUTS = ((A_WIDTH, F32), (A_WIDTH, F32), (A_WIDTH, F32),
                (MLA_HEADS * MLA_KPAD, BF16), (MLA_HEADS * MLA_KPAD, BF16), (MLA_WIDTH, BF16))


def _in_proj_tail(x, w_in_ref, w_uq_ref, w_ukv_ref, gq_ref, gkv_ref, cos_ref, sin_ref,
                  qa_ref, ka_ref, va_ref, qm_ref, km_ref, vm_ref):
    h = jnp.dot(x.astype(BF16), w_in_ref[...], preferred_element_type=F32)
    cos = cos_ref[...]
    sin = sin_ref[...]
    lane = lax.broadcasted_iota(jnp.int32, cos.shape, 1)
    first_half = (lane & (A_HEAD_DIM - 1)) < (A_HEAD_DIM // 2)

    def rope(t):
        return _rope128(t, cos, sin, first_half)

    def rope_cols(t):
        return jnp.concatenate(
            [rope(t[:, c * LANES:(c + 1) * LANES]) for c in range(t.shape[1] // LANES)], axis=1)

    qa_ref[...] = rope_cols(h[:, 0:A_WIDTH]) * A_Q_SCALE
    ka_ref[...] = rope_cols(h[:, A_WIDTH:2 * A_WIDTH])
    va_ref[...] = h[:, 2 * A_WIDTH:3 * A_WIDTH]

    c0 = 3 * A_WIDTH
    cq = _rms(h[:, c0:c0 + Q_LORA], gq_ref[...])
    q = jnp.dot(cq.astype(BF16), w_uq_ref[...], preferred_element_type=F32)
    q_parts = []
    for hh in range(MLA_HEADS):
        base = hh * MLA_KPAD
        q_parts.append(q[:, base:base + QK_NOPE])
        q_parts.append(rope(q[:, base + QK_NOPE:base + MLA_KPAD]))
    qm_ref[...] = (jnp.concatenate(q_parts, axis=1) * MLA_Q_SCALE).astype(BF16)

    c1 = c0 + Q_LORA
    ckv = _rms(h[:, c1:c1 + KV_LORA], gkv_ref[...])
    kv = jnp.dot(ckv.astype(BF16), w_ukv_ref[...], preferred_element_type=F32)
    c2 = c1 + KV_LORA
    k_pe = rope(h[:, c2:c2 + LANES])
    k_parts = []
    for hh in range(MLA_HEADS):
        k_parts.append(kv[:, hh * QK_NOPE:(hh + 1) * QK_NOPE])
        k_parts.append(k_pe)
    km_ref[...] = jnp.concatenate(k_parts, axis=1).astype(BF16)
    vm_ref[...] = kv[:, MLA_HEADS * QK_NOPE:].astype(BF16)


def _mix_ffn_head(a_ref, b_ref, x_ref, wo_ref, ga_ref, gb_ref, g1_ref, beta1_ref,
                  w1_ref, w2_ref, g2_ref, beta2_ref):
    mixed = jnp.concatenate([_rms(a_ref[...], ga_ref[...]), _rms(b_ref[...], gb_ref[...])], axis=1)
    y = jnp.dot(mixed.astype(BF16), wo_ref[...], preferred_element_type=F32)
    x1 = _layer_norm(ALPHA * x_ref[...] + y, g1_ref[...], beta1_ref[...])
    xb = x1.astype(BF16)
    acc = ALPHA * x1
    for c in range(D_FF // FF_CHUNK):
        cols = slice(c * FF_CHUNK, (c + 1) * FF_CHUNK)
        hdn = jnp.dot(xb, w1_ref[:, cols], preferred_element_type=F32)
        hdn = jnp.square(jnp.maximum(hdn, 0.0)).astype(BF16)
        acc = acc + jnp.dot(hdn, w2_ref[cols, :], preferred_element_type=F32)
    return _layer_norm(acc, g2_ref[...], beta2_ref[...])


def _permute_rows(src, o_ref, slab_sc, *, inverse):
    tm, width = o_ref.shape
    tiles = TOKEN_GROUP // SUBLANES
    if not inverse:
        for c in range(width // LANES):
            slab_sc[c] = src[:, c * LANES:(c + 1) * LANES]
    for g in range(tm // TOKEN_GROUP):
        for tile in range(tiles):
            first_token = 4 * (tile & 3) + (tile >> 2)
            natural = pl.ds(g * TOKEN_GROUP + first_token, SUBLANES, stride=tiles)
            p0 = g * TOKEN_GROUP + tile * SUBLANES
            for c in range(width // LANES):
                if inverse:
                    slab_sc[c, natural, :] = src[p0:p0 + SUBLANES, c * LANES:(c + 1) * LANES]
                else:
                    o_ref[p0:p0 + SUBLANES, c * LANES:(c + 1) * LANES] = slab_sc[c, natural, :]
    if inverse:
        for c in range(width // LANES):
            o_ref[:, c * LANES:(c + 1) * LANES] = slab_sc[c]


N_MIX_INPUTS = 12
N_IN_PROJ_INPUTS = 7


def _stage_kernel(*refs, first, last):
    refs = list(refs)
    if first:
        head_refs, refs = refs[:1], refs[1:]
    else:
        head_refs, refs = refs[:N_MIX_INPUTS], refs[N_MIX_INPUTS:]
    if not last:
        tail_in, refs = refs[:N_IN_PROJ_INPUTS], refs[N_IN_PROJ_INPUTS:]
    x_out_ref, refs = refs[0], refs[1:]
    if first:
        _permute_rows(head_refs[0], x_out_ref, refs[-1], inverse=False)
        x_new = x_out_ref[...]
    else:
        x_new = _mix_ffn_head(*head_refs)
        if last:
            _permute_rows(x_new, x_out_ref, refs[-1], inverse=True)
        else:
            x_out_ref[...] = x_new
    if not last:
        _in_proj_tail(x_new, *tail_in, *refs[:len(IN_PROJ_OUTS)])


def _stage(head_args, tail_args, *, first, last, seq, tm=STAGE_TM):
    x = head_args[0] if first else head_args[2]
    t = x.shape[0]
    row = lambda i: (i, 0)
    n_pos_blocks = seq // tm
    pos = lambda i: (i % n_pos_blocks, 0)
    if first:
        in_specs = [pl.BlockSpec((tm, D_MODEL), row)]
    else:
        in_specs = [pl.BlockSpec((tm, A_WIDTH), row), pl.BlockSpec((tm, MLA_WIDTH), row),
                    pl.BlockSpec((tm, D_MODEL), row)] + [_const_spec(c.shape) for c in head_args[3:]]
    out_specs = [pl.BlockSpec((tm, D_MODEL), row)]
    out_shape = [jax.ShapeDtypeStruct((t, D_MODEL), F32)]
    if not last:
        in_specs += [_const_spec(c.shape) for c in tail_args[:5]]
        in_specs += [pl.BlockSpec((tm, LANES), pos), pl.BlockSpec((tm, LANES), pos)]
        out_specs += [pl.BlockSpec((tm, w), row) for w, _ in IN_PROJ_OUTS]
        out_shape += [jax.ShapeDtypeStruct((t, w), dt) for w, dt in IN_PROJ_OUTS]
    scratch = [pltpu.VMEM((D_MODEL // LANES, tm, LANES), F32)] if (first or last) else []
    return pl.pallas_call(
        functools.partial(_stage_kernel, first=first, last=last),
        grid=(t // tm,),
        in_specs=in_specs, out_specs=out_specs, out_shape=out_shape, scratch_shapes=scratch,
        compiler_params=_params(1, STAGE_VMEM_LIMIT),
        name="stage_first" if first else ("stage_last" if last else "stage"),
    )(*head_args, *tail_args)


Q_TILES = 4


def _mla_kernel(q_ref, k_ref, v_ref, o_ref, m_sc, acc_sc, *, tq):
    first_diag = Q_TILES * pl.program_id(1)
    tk = tq
    m_sc[...] = jnp.full(m_sc.shape, NEG_INF, F32)
    acc_sc[...] = jnp.zeros(acc_sc.shape, F32)

    def tile(qt, k_tile, causal, q_lo=0, k_lo=0, k_n=tk):
        k_start = pl.multiple_of(k_tile * tk, tk) + k_lo
        rows = slice(q_lo, tq)
        for h in range(MLA_HEADS):
            q = q_ref[qt * tq + q_lo:(qt + 1) * tq, h * MLA_KPAD:(h + 1) * MLA_KPAD]
            k = k_ref[pl.ds(k_start, k_n), h * MLA_KPAD:(h + 1) * MLA_KPAD]
            v = v_ref[pl.ds(k_start, k_n), h * V_DIM:(h + 1) * V_DIM]
            s = lax.dot_general(q, k, (((1,), (1,)), ((), ())), preferred_element_type=F32)
            if causal is not None:
                s = jnp.where(causal, s, NEG_INF)
            m_prev = m_sc[qt, h, rows]
            m_new = jnp.maximum(m_prev, jnp.max(s, -1, keepdims=True))
            alpha = jnp.exp2(m_prev - m_new)
            p = jnp.exp2(s - jnp.concatenate([m_new] * (k_n // LANES), axis=1))
            v1 = jnp.concatenate([v, jnp.ones_like(v)], axis=1)
            acc_sc[qt, h, rows] = (jnp.concatenate([alpha, alpha], axis=1) * acc_sc[qt, h, rows]
                                   + jnp.dot(p.astype(BF16), v1, preferred_element_type=F32))
            m_sc[qt, h, rows] = m_new

    def below_diagonal(k_tile, carry):
        for qt in range(Q_TILES):
            tile(qt, k_tile, None)
        return carry

    lax.fori_loop(0, first_diag, below_diagonal, 0)

    r = lax.broadcasted_iota(jnp.int32, (tq, tk), 0)
    c = lax.broadcasted_iota(jnp.int32, (tq, tk), 1)
    causal = ((c & -TOKEN_GROUP) + _position_in_group(c)) <= ((r & -TOKEN_GROUP) + _position_in_group(r))
    half = tq // 2
    for qt in range(Q_TILES):
        for kt in range(qt):
            tile(qt, first_diag + kt, None)
        tile(qt, first_diag + qt, causal[:, :half], k_n=half)
        tile(qt, first_diag + qt, causal[half:, half:], q_lo=half, k_lo=half, k_n=half)
    for qt in range(Q_TILES):
        for h in range(MLA_HEADS):
            acc = acc_sc[qt, h]
            o_ref[qt * tq:(qt + 1) * tq, h * V_DIM:(h + 1) * V_DIM] = acc[:, :V_DIM] / acc[:, V_DIM:]


def _mla_attention(qm, km, vm, *, batch, seq, tq):
    t = qm.shape[0]
    steps = seq // (Q_TILES * tq)
    q_map = lambda b, i: (b * steps + i, 0)
    kv_map = lambda b, i: (b, 0)
    return pl.pallas_call(
        functools.partial(_mla_kernel, tq=tq),
        grid=(batch, steps),
        in_specs=[
            pl.BlockSpec((Q_TILES * tq, MLA_HEADS * MLA_KPAD), q_map),
            pl.BlockSpec((seq, MLA_HEADS * MLA_KPAD), kv_map),
            pl.BlockSpec((seq, MLA_WIDTH), kv_map),
        ],
        out_specs=pl.BlockSpec((Q_TILES * tq, MLA_WIDTH), q_map),
        out_shape=jax.ShapeDtypeStruct((t, MLA_WIDTH), F32),
        scratch_shapes=[pltpu.VMEM((Q_TILES, MLA_HEADS, tq, LANES), F32),
                        pltpu.VMEM((Q_TILES, MLA_HEADS, tq, 2 * V_DIM), F32)],
        compiler_params=_params(2),
        name="mla_attention",
    )(qm, km, vm)


HEAD_PAIRS = A_WIDTH // LANES


def _sub_index(i, dilation):
    if dilation == 16:
        return i
    if dilation == 4:
        return 32 * (i >> 5) + 4 * (i & 7) + ((i >> 3) & 3)
    return _position_in_group(i)


def _band_bias(dilation, has_prev):
    i = lax.broadcasted_iota(jnp.int32, (2 * SPAN, 2 * SPAN), 0) & (SPAN - 1)
    j = lax.broadcasted_iota(jnp.int32, (2 * SPAN, 2 * SPAN), 1)
    dist = SPAN + _sub_index(i, dilation) - ((j & SPAN) + _sub_index(j & (SPAN - 1), dilation))
    valid = (dist >= 0) & (dist <= SPAN)
    if has_prev is not None:
        valid = valid & ((j >= SPAN) | has_prev)
    return jnp.where(valid, 0.0, NEG_INF)


def _dilated_kernel(q_ref, k_ref, v_ref, o_ref, kp_sc, vp_sc, m_sc, l_sc):
    @pl.when((pl.program_id(0) == 0) & (pl.program_id(1) == 0))
    def _():
        kp_sc[...] = jnp.zeros(kp_sc.shape, F32)
        vp_sc[...] = jnp.zeros(vp_sc.shape, F32)

    has_prev = pl.program_id(1) > 0
    lane = lax.broadcasted_iota(jnp.int32, (SPAN, LANES), 1)
    head_a = lane < A_HEAD_DIM
    ones = jnp.ones((2 * SPAN, LANES), BF16)

    def rows2d(t):
        return t.reshape(-1, LANES)

    def attend(q, k, v, bias, idx, first, last):
        qa = jnp.where(head_a, q, 0.0).astype(BF16)
        qb = jnp.where(head_a, 0.0, q).astype(BF16)
        q2 = jnp.concatenate([qa, qb], axis=0)
        s = lax.dot_general(q2, k.astype(BF16), (((1,), (1,)), ((), ())), preferred_element_type=F32) + bias
        m = jnp.max(s, -1, keepdims=True)
        p = jnp.exp2(s - m)
        pv = jnp.dot(p.astype(BF16), jnp.concatenate([v.astype(BF16), ones], axis=1),
                     preferred_element_type=F32)
        acc = jnp.where(head_a, pv[:SPAN, :LANES], pv[SPAN:, :LANES])
        l = jnp.where(head_a, pv[:SPAN, LANES:], pv[SPAN:, LANES:])
        m_cur = jnp.where(head_a, jnp.broadcast_to(m[:SPAN], (SPAN, LANES)),
                          jnp.broadcast_to(m[SPAN:], (SPAN, LANES)))
        shape = o_ref[idx].shape
        if first:
            o_ref[idx] = acc.reshape(shape)
            l_sc[idx] = l.reshape(shape)
            m_sc[idx] = m_cur.reshape(shape)
            return
        m_old = rows2d(m_sc[idx])
        m_new = jnp.maximum(m_old, m_cur)
        a_old = jnp.exp2(m_old - m_new)
        a_cur = jnp.exp2(m_cur - m_new)
        acc = a_old * rows2d(o_ref[idx]) + a_cur * acc
        l = a_old * rows2d(l_sc[idx]) + a_cur * l
        if last:
            o_ref[idx] = (acc / l).reshape(shape)
        else:
            o_ref[idx] = acc.reshape(shape)
            l_sc[idx] = l.reshape(shape)
            m_sc[idx] = m_new.reshape(shape)

    every = slice(None)

    band = _band_bias(1, None)
    band_first = _band_bias(1, has_prev)
    for p in range(HEAD_PAIRS):
        cols = pl.ds(p * LANES, LANES)
        k = jnp.concatenate([rows2d(kp_sc[CHUNK_GROUPS - 1, :, :, cols]), rows2d(k_ref[0, :, :, cols])], axis=0)
        v = jnp.concatenate([rows2d(vp_sc[CHUNK_GROUPS - 1, :, :, cols]), rows2d(v_ref[0, :, :, cols])], axis=0)
        attend(rows2d(q_ref[0, :, :, cols]), k, v, band_first, (0, every, every, cols), True, False)

    def d1_body(g, carry):
        for p in range(HEAD_PAIRS):
            cols = pl.ds(p * LANES, LANES)
            attend(rows2d(q_ref[g, :, :, cols]), rows2d(k_ref[pl.ds(g - 1, 2), :, :, cols]),
                   rows2d(v_ref[pl.ds(g - 1, 2), :, :, cols]), band, (g, every, every, cols), True, False)
        return carry
    lax.fori_loop(1, CHUNK_GROUPS, d1_body, 0, unroll=True)

    band = _band_bias(4, None)
    band_first = _band_bias(4, has_prev)

    def d4_first(r4, carry):
        res = pl.ds(4 * r4, 4)
        for p in range(HEAD_PAIRS):
            cols = pl.ds(p * LANES, LANES)
            k = jnp.concatenate([rows2d(kp_sc[CHUNK_GROUPS - 4:, res, :, cols]), rows2d(k_ref[0:4, res, :, cols])],
                                axis=0)
            v = jnp.concatenate([rows2d(vp_sc[CHUNK_GROUPS - 4:, res, :, cols]), rows2d(v_ref[0:4, res, :, cols])],
                                axis=0)
            attend(rows2d(q_ref[0:4, res, :, cols]), k, v, band_first, (slice(0, 4), res, every, cols), False, False)
        return carry
    lax.fori_loop(0, 4, d4_first, 0, unroll=True)

    blocks_after_first = CHUNK_GROUPS // 4 - 1

    def d4_body(t, carry):
        res = pl.ds(4 * (t // blocks_after_first), 4)
        g0 = 4 * (1 + t % blocks_after_first)
        for p in range(HEAD_PAIRS):
            cols = pl.ds(p * LANES, LANES)
            attend(rows2d(q_ref[pl.ds(g0, 4), res, :, cols]), rows2d(k_ref[pl.ds(g0 - 4, 8), res, :, cols]),
                   rows2d(v_ref[pl.ds(g0 - 4, 8), res, :, cols]), band, (pl.ds(g0, 4), res, every, cols),
                   False, False)
        return carry
    lax.fori_loop(0, 4 * blocks_after_first, d4_body, 0, unroll=True)

    band16 = _band_bias(16, has_prev)

    def d16_body(r, carry):
        for p in range(HEAD_PAIRS):
            cols = pl.ds(p * LANES, LANES)
            k = jnp.concatenate([rows2d(kp_sc[:, r, :, cols]), rows2d(k_ref[:, r, :, cols])], axis=0)
            v = jnp.concatenate([rows2d(vp_sc[:, r, :, cols]), rows2d(v_ref[:, r, :, cols])], axis=0)
            attend(rows2d(q_ref[:, r, :, cols]), k, v, band16, (every, r, every, cols), False, True)
        return carry
    lax.fori_loop(0, MAX_DILATION, d16_body, 0, unroll=True)

    kp_sc[...] = k_ref[...]
    vp_sc[...] = v_ref[...]


def _dilated_attention(qa, ka, va, *, batch, seq):
    t = qa.shape[0]
    chunks = seq // (CHUNK_GROUPS * TOKEN_GROUP)
    tiles = TOKEN_GROUP // SUBLANES
    shape4 = (t // TOKEN_GROUP, tiles, SUBLANES, A_WIDTH)
    view = lambda a: a.reshape(shape4)
    block = (CHUNK_GROUPS, tiles, SUBLANES, A_WIDTH)
    spec = pl.BlockSpec(block, lambda b, c: (b * chunks + c, 0, 0, 0))
    scratch = pltpu.VMEM(block, F32)
    out = pl.pallas_call(
        _dilated_kernel,
        grid=(batch, chunks),
        in_specs=[spec, spec, spec],
        out_specs=spec,
        out_shape=jax.ShapeDtypeStruct(shape4, F32),
        scratch_shapes=[scratch, scratch, scratch, scratch],
        compiler_params=_params(2),
        name="dilated",
    )(view(qa), view(ka), view(va))
    return out.reshape(t, A_WIDTH)


def _rope_tables(seq):
    half = A_HEAD_DIM // 2
    inv_freq = ROPE_THETA ** (-jnp.arange(half, dtype=F32) / half)
    row = jnp.arange(seq, dtype=jnp.int32)
    pos = (row & -TOKEN_GROUP) + _position_in_group(row)
    ang = pos.astype(F32)[:, None] * inv_freq[None, :]
    cos = jnp.cos(ang)
    sin = jnp.sin(ang)
    cos128 = jnp.tile(cos, (1, LANES // half))
    sin128 = jnp.tile(jnp.concatenate([-sin, sin], axis=1), (1, LANES // A_HEAD_DIM))
    return cos128, sin128


def _prep_w_uq(w):
    w = w.reshape(-1, Q_LORA, MLA_HEADS, QK_NOPE + QK_ROPE)
    w = jnp.pad(w, ((0, 0), (0, 0), (0, 0), (0, MLA_KPAD - QK_NOPE - QK_ROPE)))
    return w.reshape(-1, Q_LORA, MLA_HEADS * MLA_KPAD).astype(BF16)


def _prep_w_ukv(w):
    w = w.reshape(-1, KV_LORA, MLA_HEADS, QK_NOPE + V_DIM)
    k = w[..., :QK_NOPE].reshape(-1, KV_LORA, MLA_HEADS * QK_NOPE)
    v = w[..., QK_NOPE:].reshape(-1, KV_LORA, MLA_HEADS * V_DIM)
    return jnp.concatenate([k, v], axis=-1).astype(BF16)


def kernel(x, w_in, q_a_norm, kv_a_norm, w_uq, w_ukv, a_out_norm, b_out_norm, w_o,
           ln1_g, ln1_b, w_ff1, w_ff2, ln2_g, ln2_b):
    batch, seq, _ = x.shape
    assert seq % (CHUNK_GROUPS * TOKEN_GROUP) == 0
    t = batch * seq
    cos, sin = _rope_tables(seq)
    row2d = lambda v: v.reshape(1, -1)
    w_in_b = jnp.pad(w_in, ((0, 0), (0, 0), (0, IN_COLS_PAD - w_in.shape[2]))).astype(BF16)
    w_uq_b = _prep_w_uq(w_uq)
    w_ukv_b = _prep_w_ukv(w_ukv)
    w_o_b = w_o.astype(BF16)
    w_ff1_b = w_ff1.astype(BF16)
    w_ff2_b = w_ff2.astype(BF16)

    def in_proj_args(l):
        return (w_in_b[l], w_uq_b[l], w_ukv_b[l], row2d(q_a_norm[l]), row2d(kv_a_norm[l]), cos, sin)

    def mix_args(l):
        return (w_o_b[l], row2d(a_out_norm[l]), row2d(b_out_norm[l]), row2d(ln1_g[l]),
                row2d(ln1_b[l]), w_ff1_b[l], w_ff2_b[l], row2d(ln2_g[l]), row2d(ln2_b[l]))

    xf, qa, ka, va, qm, km, vm = _stage((x.reshape(t, D_MODEL),), in_proj_args(0),
                                        first=True, last=False, seq=seq)
    for l in range(DEPTH):
        b_out = _mla_attention(qm, km, vm, batch=batch, seq=seq, tq=512)
        a_out = _dilated_attention(qa, ka, va, batch=batch, seq=seq)
        head = (a_out, b_out, xf) + mix_args(l)
        if l + 1 < DEPTH:
            xf, qa, ka, va, qm, km, vm = _stage(head, in_proj_args(l + 1), first=False, last=False, seq=seq)
        else:
            (xf,) = _stage(head, (), first=False, last=True, seq=seq)
    return xf.reshape(batch, seq, D_MODEL)
```

```python
import functools

import jax
import jax.numpy as jnp
from jax import lax
from jax.experimental import pallas as pl
from jax.experimental.pallas import tpu as pltpu

D_MODEL = 1024
DEPTH = 4
A_HEADS = 8
A_HEAD_DIM = 64
A_WIDTH = A_HEADS * A_HEAD_DIM
MLA_HEADS = 4
QK_NOPE = 128
QK_ROPE = 64
V_DIM = 128
Q_LORA = 256
KV_LORA = 128
MLA_WIDTH = MLA_HEADS * V_DIM
D_FF = 4 * D_MODEL
ROPE_THETA = 10000.0
ALPHA = (2.0 * DEPTH) ** 0.25
LN_EPS = 1e-5
RMS_EPS = 1e-6

LANES = 128
SUBLANES = 8
MLA_KPAD = 256
IN_COLS_PAD = 3 * A_WIDTH + Q_LORA + KV_LORA + LANES
VMEM_LIMIT = 56 * 1024 * 1024
STAGE_VMEM_LIMIT = 60 * 1024 * 1024
STAGE_TM = 512

SPAN = 128
MAX_DILATION = 16
TOKEN_GROUP = SPAN
CHUNK_GROUPS = SPAN * MAX_DILATION // TOKEN_GROUP

F32 = jnp.float32
BF16 = jnp.bfloat16
NEG_INF = float("-inf")
LOG2E = 1.4426950408889634
MLA_Q_SCALE = (QK_NOPE + QK_ROPE) ** -0.5 * LOG2E
A_Q_SCALE = A_HEAD_DIM ** -0.5 * LOG2E


def _const_spec(shape):
    return pl.BlockSpec(shape, lambda *_: (0,) * len(shape), pipeline_mode=pl.Buffered(1))


def _params(n_axes, vmem_limit=VMEM_LIMIT):
    return pltpu.CompilerParams(dimension_semantics=("arbitrary",) * n_axes,
                                vmem_limit_bytes=vmem_limit)


def _rms(x, g):
    return x * lax.rsqrt(jnp.mean(x * x, -1, keepdims=True) + RMS_EPS) * g


def _layer_norm(z, g, b):
    mu = jnp.mean(z, -1, keepdims=True)
    zc = z - mu
    var = jnp.mean(zc * zc, -1, keepdims=True)
    return zc * lax.rsqrt(var + LN_EPS) * g + b


def _rope128(x, cos, sin_signed, first_half):
    rot = jnp.where(first_half, pltpu.roll(x, 96, 1), pltpu.roll(x, 32, 1))
    return x * cos + rot * sin_signed


def _position_in_group(i):
    return 16 * (i & 7) + 4 * ((i >> 3) & 3) + ((i >> 5) & 3)


FF_CHUNK = 1024
IN_PROJ_OUTS = ((A_WIDTH, F32), (A_WIDTH, F32), (A_WIDTH, F32),
                (MLA_HEADS * MLA_KPAD, BF16), (MLA_HEADS * MLA_KPAD, BF16), (MLA_WIDTH, BF16))


def _in_proj_tail(x, w_in_ref, w_uq_ref, w_ukv_ref, gq_ref, gkv_ref, cos_ref, sin_ref,
                  qa_ref, ka_ref, va_ref, qm_ref, km_ref, vm_ref):
    h = jnp.dot(x.astype(BF16), w_in_ref[...], preferred_element_type=F32)
    cos = cos_ref[...]
    sin = sin_ref[...]
    lane = lax.broadcasted_iota(jnp.int32, cos.shape, 1)
    first_half = (lane & (A_HEAD_DIM - 1)) < (A_HEAD_DIM // 2)

    def rope(t):
        return _rope128(t, cos, sin, first_half)

    def rope_cols(t):
        return jnp.concatenate(
            [rope(t[:, c * LANES:(c + 1) * LANES]) for c in range(t.shape[1] // LANES)], axis=1)

    qa_ref[...] = rope_cols(h[:, 0:A_WIDTH]) * A_Q_SCALE
    ka_ref[...] = rope_cols(h[:, A_WIDTH:2 * A_WIDTH])
    va_ref[...] = h[:, 2 * A_WIDTH:3 * A_WIDTH]

    c0 = 3 * A_WIDTH
    cq = _rms(h[:, c0:c0 + Q_LORA], gq_ref[...])
    q = jnp.dot(cq.astype(BF16), w_uq_ref[...], preferred_element_type=F32)
    q_parts = []
    for hh in range(MLA_HEADS):
        base = hh * MLA_KPAD
        q_parts.append(q[:, base:base + QK_NOPE])
        q_parts.append(rope(q[:, base + QK_NOPE:base + MLA_KPAD]))
    qm_ref[...] = (jnp.concatenate(q_parts, axis=1) * MLA_Q_SCALE).astype(BF16)

    c1 = c0 + Q_LORA
    ckv = _rms(h[:, c1:c1 + KV_LORA], gkv_ref[...])
    kv = jnp.dot(ckv.astype(BF16), w_ukv_ref[...], preferred_element_type=F32)
    c2 = c1 + KV_LORA
    k_pe = rope(h[:, c2:c2 + LANES])
    k_parts = []
    for hh in range(MLA_HEADS):
        k_parts.append(kv[:, hh * QK_NOPE:(hh + 1) * QK_NOPE])
        k_parts.append(k_pe)
    km_ref[...] = jnp.concatenate(k_parts, axis=1).astype(BF16)
    vm_ref[...] = kv[:, MLA_HEADS * QK_NOPE:].astype(BF16)


def _mix_ffn_head(a_ref, b_ref, x_ref, wo_ref, ga_ref, gb_ref, g1_ref, beta1_ref,
                  w1_ref, w2_ref, g2_ref, beta2_ref):
    mixed = jnp.concatenate([_rms(a_ref[...], ga_ref[...]), _rms(b_ref[...], gb_ref[...])], axis=1)
    y = jnp.dot(mixed.astype(BF16), wo_ref[...], preferred_element_type=F32)
    x1 = _layer_norm(ALPHA * x_ref[...] + y, g1_ref[...], beta1_ref[...])
    xb = x1.astype(BF16)
    acc = ALPHA * x1
    for c in range(D_FF // FF_CHUNK):
        cols = slice(c * FF_CHUNK, (c + 1) * FF_CHUNK)
        hdn = jnp.dot(xb, w1_ref[:, cols], preferred_element_type=F32)
        hdn = jnp.square(jnp.maximum(hdn, 0.0)).astype(BF16)
        acc = acc + jnp.dot(hdn, w2_ref[cols, :], preferred_element_type=F32)
    return _layer_norm(acc, g2_ref[...], beta2_ref[...])


def _permute_rows(src, o_ref, slab_sc, *, inverse):
    tm, width = o_ref.shape
    tiles = TOKEN_GROUP // SUBLANES
    if not inverse:
        for c in range(width // LANES):
            slab_sc[c] = src[:, c * LANES:(c + 1) * LANES]
    for g in range(tm // TOKEN_GROUP):
        for tile in range(tiles):
            first_token = 4 * (tile & 3) + (tile >> 2)
            natural = pl.ds(g * TOKEN_GROUP + first_token, SUBLANES, stride=tiles)
            p0 = g * TOKEN_GROUP + tile * SUBLANES
            for c in range(width // LANES):
                if inverse:
                    slab_sc[c, natural, :] = src[p0:p0 + SUBLANES, c * LANES:(c + 1) * LANES]
                else:
                    o_ref[p0:p0 + SUBLANES, c * LANES:(c + 1) * LANES] = slab_sc[c, natural, :]
    if inverse:
        for c in range(width // LANES):
            o_ref[:, c * LANES:(c + 1) * LANES] = slab_sc[c]


N_MIX_INPUTS = 12
N_IN_PROJ_INPUTS = 7


def _stage_kernel(*refs, first, last):
    refs = list(refs)
    if first:
        head_refs, refs = refs[:1], refs[1:]
    else:
        head_refs, refs = refs[:N_MIX_INPUTS], refs[N_MIX_INPUTS:]
    if not last:
        tail_in, refs = refs[:N_IN_PROJ_INPUTS], refs[N_IN_PROJ_INPUTS:]
    x_out_ref, refs = refs[0], refs[1:]
    if first:
        _permute_rows(head_refs[0], x_out_ref, refs[-1], inverse=False)
        _in_proj_tail(x_out_ref[...], *tail_in, *refs[:len(IN_PROJ_OUTS)])
    elif last:
        _permute_rows(_mix_ffn_head(*head_refs), x_out_ref, refs[-1], inverse=True)
    else:
        x_prev_sc = refs[-1]

        @pl.when(pl.program_id(0) == 0)
        def _():
            x_prev_sc[...] = jnp.zeros(x_prev_sc.shape, F32)

        _in_proj_tail(x_prev_sc[...], *tail_in, *refs[:len(IN_PROJ_OUTS)])
        x_new = _mix_ffn_head(*head_refs)
        x_out_ref[...] = x_new
        x_prev_sc[...] = x_new


def _stage(head_args, tail_args, *, first, last, seq, tm=STAGE_TM):
    x = head_args[0] if first else head_args[2]
    t = x.shape[0]
    n = t // tm
    lagged = not (first or last)
    steps = n + 1 if lagged else n
    row = lambda i: (jnp.minimum(i, n - 1), 0)
    tail_row = (lambda i: (jnp.maximum(i - 1, 0), 0)) if lagged else row
    n_pos_blocks = seq // tm
    pos = lambda i: (tail_row(i)[0] % n_pos_blocks, 0)
    if first:
        in_specs = [pl.BlockSpec((tm, D_MODEL), row)]
    else:
        in_specs = [pl.BlockSpec((tm, A_WIDTH), row), pl.BlockSpec((tm, MLA_WIDTH), row),
                    pl.BlockSpec((tm, D_MODEL), row)] + [_const_spec(c.shape) for c in head_args[3:]]
    out_specs = [pl.BlockSpec((tm, D_MODEL), row)]
    out_shape = [jax.ShapeDtypeStruct((t, D_MODEL), F32)]
    if not last:
        in_specs += [_const_spec(c.shape) for c in tail_args[:5]]
        in_specs += [pl.BlockSpec((tm, LANES), pos), pl.BlockSpec((tm, LANES), pos)]
        out_specs += [pl.BlockSpec((tm, w), tail_row) for w, _ in IN_PROJ_OUTS]
        out_shape += [jax.ShapeDtypeStruct((t, w), dt) for w, dt in IN_PROJ_OUTS]
    if lagged:
        scratch = [pltpu.VMEM((tm, D_MODEL), F32)]
    else:
        scratch = [pltpu.VMEM((D_MODEL // LANES, tm, LANES), F32)]
    return pl.pallas_call(
        functools.partial(_stage_kernel, first=first, last=last),
        grid=(steps,),
        in_specs=in_specs, out_specs=out_specs, out_shape=out_shape, scratch_shapes=scratch,
        compiler_params=_params(1, STAGE_VMEM_LIMIT),
        name="stage_first" if first else ("stage_last" if last else "stage"),
    )(*head_args, *tail_args)


Q_TILES = 4


def _mla_kernel(q_ref, k_ref, v_ref, o_ref, m_sc, acc_sc, *, tq):
    first_diag = Q_TILES * pl.program_id(1)
    tk = tq
    m_sc[...] = jnp.full(m_sc.shape, NEG_INF, F32)
    acc_sc[...] = jnp.zeros(acc_sc.shape, F32)

    def tile(qt, k_tile, causal, q_lo=0, k_lo=0, k_n=tk):
        k_start = pl.multiple_of(k_tile * tk, tk) + k_lo
        rows = slice(q_lo, tq)
        for h in range(MLA_HEADS):
            q = q_ref[qt * tq + q_lo:(qt + 1) * tq, h * MLA_KPAD:(h + 1) * MLA_KPAD]
            k = k_ref[pl.ds(k_start, k_n), h * MLA_KPAD:(h + 1) * MLA_KPAD]
            v = v_ref[pl.ds(k_start, k_n), h * V_DIM:(h + 1) * V_DIM]
            s = lax.dot_general(q, k, (((1,), (1,)), ((), ())), preferred_element_type=F32)
            if causal is not None:
                s = jnp.where(causal, s, NEG_INF)
            m_prev = m_sc[qt, h, rows]
            m_new = jnp.maximum(m_prev, jnp.max(s, -1, keepdims=True))
            alpha = jnp.exp2(m_prev - m_new)
            p = jnp.exp2(s - jnp.concatenate([m_new] * (k_n // LANES), axis=1))
            v1 = jnp.concatenate([v, jnp.ones_like(v)], axis=1)
            acc_sc[qt, h, rows] = (jnp.concatenate([alpha, alpha], axis=1) * acc_sc[qt, h, rows]
                                   + jnp.dot(p.astype(BF16), v1, preferred_element_type=F32))
            m_sc[qt, h, rows] = m_new

    def below_diagonal(k_tile, carry):
        for qt in range(Q_TILES):
            tile(qt, k_tile, None)
        return carry

    lax.fori_loop(0, first_diag, below_diagonal, 0)

    r = lax.broadcasted_iota(jnp.int32, (tq, tk), 0)
    c = lax.broadcasted_iota(jnp.int32, (tq, tk), 1)
    causal = ((c & -TOKEN_GROUP) + _position_in_group(c)) <= ((r & -TOKEN_GROUP) + _position_in_group(r))
    half = tq // 2
    for qt in range(Q_TILES):
        for kt in range(qt):
            tile(qt, first_diag + kt, None)
        tile(qt, first_diag + qt, causal[:, :half], k_n=half)
        tile(qt, first_diag + qt, causal[half:, half:], q_lo=half, k_lo=half, k_n=half)
    for qt in range(Q_TILES):
        for h in range(MLA_HEADS):
            acc = acc_sc[qt, h]
            o_ref[qt * tq:(qt + 1) * tq, h * V_DIM:(h + 1) * V_DIM] = acc[:, :V_DIM] / acc[:, V_DIM:]


def _mla_attention(qm, km, vm, *, batch, seq, tq):
    t = qm.shape[0]
    steps = seq // (Q_TILES * tq)
    q_map = lambda b, i: (b * steps + i, 0)
    kv_map = lambda b, i: (b, 0)
    return pl.pallas_call(
        functools.partial(_mla_kernel, tq=tq),
        grid=(batch, steps),
        in_specs=[
            pl.BlockSpec((Q_TILES * tq, MLA_HEADS * MLA_KPAD), q_map),
            pl.BlockSpec((seq, MLA_HEADS * MLA_KPAD), kv_map),
            pl.BlockSpec((seq, MLA_WIDTH), kv_map),
        ],
        out_specs=pl.BlockSpec((Q_TILES * tq, MLA_WIDTH), q_map),
        out_shape=jax.ShapeDtypeStruct((t, MLA_WIDTH), F32),
        scratch_shapes=[pltpu.VMEM((Q_TILES, MLA_HEADS, tq, LANES), F32),
                        pltpu.VMEM((Q_TILES, MLA_HEADS, tq, 2 * V_DIM), F32)],
        compiler_params=_params(2),
        name="mla_attention",
    )(qm, km, vm)


HEAD_PAIRS = A_WIDTH // LANES


def _sub_index(i, dilation):
    if dilation == 16:
        return i
    if dilation == 4:
        return 32 * (i >> 5) + 4 * (i & 7) + ((i >> 3) & 3)
    return _position_in_group(i)


def _band_bias(dilation, has_prev):
    i = lax.broadcasted_iota(jnp.int32, (2 * SPAN, 2 * SPAN), 0) & (SPAN - 1)
    j = lax.broadcasted_iota(jnp.int32, (2 * SPAN, 2 * SPAN), 1)
    dist = SPAN + _sub_index(i, dilation) - ((j & SPAN) + _sub_index(j & (SPAN - 1), dilation))
    valid = (dist >= 0) & (dist <= SPAN)
    if has_prev is not None:
        valid = valid & ((j >= SPAN) | has_prev)
    return jnp.where(valid, 0.0, NEG_INF)


def _dilated_kernel(q_ref, k_ref, v_ref, o_ref, kp_sc, vp_sc, m_sc, l_sc):
    @pl.when((pl.program_id(0) == 0) & (pl.program_id(1) == 0))
    def _():
        kp_sc[...] = jnp.zeros(kp_sc.shape, F32)
        vp_sc[...] = jnp.zeros(vp_sc.shape, F32)

    has_prev = pl.program_id(1) > 0
    lane = lax.broadcasted_iota(jnp.int32, (SPAN, LANES), 1)
    head_a = lane < A_HEAD_DIM
    ones = jnp.ones((2 * SPAN, LANES), BF16)

    def rows2d(t):
        return t.reshape(-1, LANES)

    def attend(q, k, v, bias, idx, first, last):
        qa = jnp.where(head_a, q, 0.0).astype(BF16)
        qb = jnp.where(head_a, 0.0, q).astype(BF16)
        q2 = jnp.concatenate([qa, qb], axis=0)
        s = lax.dot_general(q2, k.astype(BF16), (((1,), (1,)), ((), ())), preferred_element_type=F32) + bias
        m = jnp.max(s, -1, keepdims=True)
        p = jnp.exp2(s - m)
        pv = jnp.dot(p.astype(BF16), jnp.concatenate([v.astype(BF16), ones], axis=1),
                     preferred_element_type=F32)
        acc = jnp.where(head_a, pv[:SPAN, :LANES], pv[SPAN:, :LANES])
        l = jnp.where(head_a, pv[:SPAN, LANES:], pv[SPAN:, LANES:])
        m_cur = jnp.where(head_a, jnp.broadcast_to(m[:SPAN], (SPAN, LANES)),
                          jnp.broadcast_to(m[SPAN:], (SPAN, LANES)))
        shape = o_ref[idx].shape
        if first:
            o_ref[idx] = acc.reshape(shape)
            l_sc[idx] = l.reshape(shape)
            m_sc[idx] = m_cur.reshape(shape)
            return
        m_old = rows2d(m_sc[idx])
        m_new = jnp.maximum(m_old, m_cur)
        a_old = jnp.exp2(m_old - m_new)
        a_cur = jnp.exp2(m_cur - m_new)
        acc = a_old * rows2d(o_ref[idx]) + a_cur * acc
        l = a_old * rows2d(l_sc[idx]) + a_cur * l
        if last:
            o_ref[idx] = (acc / l).reshape(shape)
        else:
            o_ref[idx] = acc.reshape(shape)
            l_sc[idx] = l.reshape(shape)
            m_sc[idx] = m_new.reshape(shape)

    every = slice(None)

    band = _band_bias(1, None)
    band_first = _band_bias(1, has_prev)
    for p in range(HEAD_PAIRS):
        cols = pl.ds(p * LANES, LANES)
        k = jnp.concatenate([rows2d(kp_sc[CHUNK_GROUPS - 1, :, :, cols]), rows2d(k_ref[0, :, :, cols])], axis=0)
        v = jnp.concatenate([rows2d(vp_sc[CHUNK_GROUPS - 1, :, :, cols]), rows2d(v_ref[0, :, :, cols])], axis=0)
        attend(rows2d(q_ref[0, :, :, cols]), k, v, band_first, (0, every, every, cols), True, False)

    def d1_body(g, carry):
        for p in range(HEAD_PAIRS):
            cols = pl.ds(p * LANES, LANES)
            attend(rows2d(q_ref[g, :, :, cols]), rows2d(k_ref[pl.ds(g - 1, 2), :, :, cols]),
                   rows2d(v_ref[pl.ds(g - 1, 2), :, :, cols]), band, (g, every, every, cols), True, False)
        return carry
    lax.fori_loop(1, CHUNK_GROUPS, d1_body, 0, unroll=True)

    band = _band_bias(4, None)
    band_first = _band_bias(4, has_prev)

    def d4_first(r4, carry):
        res = pl.ds(4 * r4, 4)
        for p in range(HEAD_PAIRS):
            cols = pl.ds(p * LANES, LANES)
            k = jnp.concatenate([rows2d(kp_sc[CHUNK_GROUPS - 4:, res, :, cols]), rows2d(k_ref[0:4, res, :, cols])],
                                axis=0)
            v = jnp.concatenate([rows2d(vp_sc[CHUNK_GROUPS - 4:, res, :, cols]), rows2d(v_ref[0:4, res, :, cols])],
                                axis=0)
            attend(rows2d(q_ref[0:4, res, :, cols]), k, v, band_first, (slice(0, 4), res, every, cols), False, False)
        return carry
    lax.fori_loop(0, 4, d4_first, 0, unroll=True)

    blocks_after_first = CHUNK_GROUPS // 4 - 1

    def d4_body(t, carry):
        res = pl.ds(4 * (t // blocks_after_first), 4)
        g0 = 4 * (1 + t % blocks_after_first)
        for p in range(HEAD_PAIRS):
            cols = pl.ds(p * LANES, LANES)
            attend(rows2d(q_ref[pl.ds(g0, 4), res, :, cols]), rows2d(k_ref[pl.ds(g0 - 4, 8), res, :, cols]),
                   rows2d(v_ref[pl.ds(g0 - 4, 8), res, :, cols]), band, (pl.ds(g0, 4), res, every, cols),
                   False, False)
        return carry
    lax.fori_loop(0, 4 * blocks_after_first, d4_body, 0, unroll=True)

    band16 = _band_bias(16, has_prev)

    def d16_body(r, carry):
        for p in range(HEAD_PAIRS):
            cols = pl.ds(p * LANES, LANES)
            k = jnp.concatenate([rows2d(kp_sc[:, r, :, cols]), rows2d(k_ref[:, r, :, cols])], axis=0)
            v = jnp.concatenate([rows2d(vp_sc[:, r, :, cols]), rows2d(v_ref[:, r, :, cols])], axis=0)
            attend(rows2d(q_ref[:, r, :, cols]), k, v, band16, (every, r, every, cols), False, True)
        return carry
    lax.fori_loop(0, MAX_DILATION, d16_body, 0, unroll=True)

    kp_sc[...] = k_ref[...]
    vp_sc[...] = v_ref[...]


def _dilated_attention(qa, ka, va, *, batch, seq):
    t = qa.shape[0]
    chunks = seq // (CHUNK_GROUPS * TOKEN_GROUP)
    tiles = TOKEN_GROUP // SUBLANES
    shape4 = (t // TOKEN_GROUP, tiles, SUBLANES, A_WIDTH)
    view = lambda a: a.reshape(shape4)
    block = (CHUNK_GROUPS, tiles, SUBLANES, A_WIDTH)
    spec = pl.BlockSpec(block, lambda b, c: (b * chunks + c, 0, 0, 0))
    scratch = pltpu.VMEM(block, F32)
    out = pl.pallas_call(
        _dilated_kernel,
        grid=(batch, chunks),
        in_specs=[spec, spec, spec],
        out_specs=spec,
        out_shape=jax.ShapeDtypeStruct(shape4, F32),
        scratch_shapes=[scratch, scratch, scratch, scratch],
        compiler_params=_params(2),
        name="dilated",
    )(view(qa), view(ka), view(va))
    return out.reshape(t, A_WIDTH)


def _rope_tables(seq):
    half = A_HEAD_DIM // 2
    inv_freq = ROPE_THETA ** (-jnp.arange(half, dtype=F32) / half)
    row = jnp.arange(seq, dtype=jnp.int32)
    pos = (row & -TOKEN_GROUP) + _position_in_group(row)
    ang = pos.astype(F32)[:, None] * inv_freq[None, :]
    cos = jnp.cos(ang)
    sin = jnp.sin(ang)
    cos128 = jnp.tile(cos, (1, LANES // half))
    sin128 = jnp.tile(jnp.concatenate([-sin, sin], axis=1), (1, LANES // A_HEAD_DIM))
    return cos128, sin128


def _prep_w_uq(w):
    w = w.reshape(Q_LORA, MLA_HEADS, QK_NOPE + QK_ROPE)
    w = jnp.pad(w, ((0, 0), (0, 0), (0, MLA_KPAD - QK_NOPE - QK_ROPE)))
    return w.reshape(Q_LORA, MLA_HEADS * MLA_KPAD).astype(BF16)


def _prep_w_ukv(w):
    w = w.reshape(KV_LORA, MLA_HEADS, QK_NOPE + V_DIM)
    k = w[:, :, :QK_NOPE].reshape(KV_LORA, MLA_HEADS * QK_NOPE)
    v = w[:, :, QK_NOPE:].reshape(KV_LORA, MLA_HEADS * V_DIM)
    return jnp.concatenate([k, v], axis=1).astype(BF16)


def kernel(x, w_in, q_a_norm, kv_a_norm, w_uq, w_ukv, a_out_norm, b_out_norm, w_o,
           ln1_g, ln1_b, w_ff1, w_ff2, ln2_g, ln2_b):
    batch, seq, _ = x.shape
    assert seq % (CHUNK_GROUPS * TOKEN_GROUP) == 0
    t = batch * seq
    cos, sin = _rope_tables(seq)
    row2d = lambda v: v.reshape(1, -1)

    def in_proj_args(l):
        w_in_l = jnp.pad(w_in[l], ((0, 0), (0, IN_COLS_PAD - w_in.shape[2]))).astype(BF16)
        return (w_in_l, _prep_w_uq(w_uq[l]), _prep_w_ukv(w_ukv[l]), row2d(q_a_norm[l]), row2d(kv_a_norm[l]),
                cos, sin)

    def mix_args(l):
        return (w_o[l].astype(BF16), row2d(a_out_norm[l]), row2d(b_out_norm[l]), row2d(ln1_g[l]),
                row2d(ln1_b[l]), w_ff1[l].astype(BF16), w_ff2[l].astype(BF16), row2d(ln2_g[l]), row2d(ln2_b[l]))

    xf, qa, ka, va, qm, km, vm = _stage((x.reshape(t, D_MODEL),), in_proj_args(0),
                                        first=True, last=False, seq=seq)
    for l in range(DEPTH):
        b_out = _mla_attention(qm, km, vm, batch=batch, seq=seq, tq=512)
        a_out = _dilated_attention(qa, ka, va, batch=batch, seq=seq)
        head = (a_out, b_out, xf) + mix_args(l)
        if l + 1 < DEPTH:
            xf, qa, ka, va, qm, km, vm = _stage(head, in_proj_args(l + 1), first=False, last=False, seq=seq)
        else:
            (xf,) = _stage(head, (), first=False, last=True, seq=seq)
    return xf.reshape(batch, seq, D_MODEL)
```

```python
import functools

import jax
import jax.numpy as jnp
from jax import lax
from jax.experimental import pallas as pl
from jax.experimental.pallas import tpu as pltpu

D_MODEL = 1024
DEPTH = 4
A_HEADS = 8
A_HEAD_DIM = 64
A_WIDTH = A_HEADS * A_HEAD_DIM
MLA_HEADS = 4
QK_NOPE = 128
QK_ROPE = 64
V_DIM = 128
Q_LORA = 256
KV_LORA = 128
MLA_WIDTH = MLA_HEADS * V_DIM
D_FF = 4 * D_MODEL
ROPE_THETA = 10000.0
ALPHA = (2.0 * DEPTH) ** 0.25
LN_EPS = 1e-5
RMS_EPS = 1e-6

LANES = 128
SUBLANES = 8
MLA_KPAD = 256
IN_COLS_PAD = 3 * A_WIDTH + Q_LORA + KV_LORA + LANES
VMEM_LIMIT = 56 * 1024 * 1024
STAGE_VMEM_LIMIT = 60 * 1024 * 1024
STAGE_TM = 512

SPAN = 128
MAX_DILATION = 16
TOKEN_GROUP = SPAN
CHUNK_GROUPS = SPAN * MAX_DILATION // TOKEN_GROUP

F32 = jnp.float32
BF16 = jnp.bfloat16
NEG_INF = float("-inf")
LOG2E = 1.4426950408889634
MLA_Q_SCALE = (QK_NOPE + QK_ROPE) ** -0.5 * LOG2E
A_Q_SCALE = A_HEAD_DIM ** -0.5 * LOG2E


def _const_spec(shape):
    return pl.BlockSpec(shape, lambda *_: (0,) * len(shape), pipeline_mode=pl.Buffered(1))


def _params(n_axes, vmem_limit=VMEM_LIMIT):
    return pltpu.CompilerParams(dimension_semantics=("arbitrary",) * n_axes,
                                vmem_limit_bytes=vmem_limit)


def _rms(x, g):
    return x * lax.rsqrt(jnp.mean(x * x, -1, keepdims=True) + RMS_EPS) * g


def _layer_norm(z, g, b):
    mu = jnp.mean(z, -1, keepdims=True)
    zc = z - mu
    var = jnp.mean(zc * zc, -1, keepdims=True)
    return zc * lax.rsqrt(var + LN_EPS) * g + b


def _rope128(x, cos, sin_signed, first_half):
    rot = jnp.where(first_half, pltpu.roll(x, 96, 1), pltpu.roll(x, 32, 1))
    return x * cos + rot * sin_signed


def _position_in_group(i):
    return 16 * (i & 7) + 4 * ((i >> 3) & 3) + ((i >> 5) & 3)


FF_CHUNK = 1024
IN_PROJ_OUTS = ((A_WIDTH, F32), (A_WIDTH, F32), (A_WIDTH, F32),
                (MLA_HEADS * MLA_KPAD, BF16), (MLA_HEADS * MLA_KPAD, BF16), (MLA_WIDTH, BF16))


def _in_proj_tail(x, w_in_ref, w_uq_ref, w_ukv_ref, gq_ref, gkv_ref, cos_ref, sin_ref,
                  qa_ref, ka_ref, va_ref, qm_ref, km_ref, vm_ref):
    h = jnp.dot(x.astype(BF16), w_in_ref[...], preferred_element_type=F32)
    cos = cos_ref[...]
    sin = sin_ref[...]
    lane = lax.broadcasted_iota(jnp.int32, cos.shape, 1)
    first_half = (lane & (A_HEAD_DIM - 1)) < (A_HEAD_DIM // 2)

    def rope(t):
        return _rope128(t, cos, sin, first_half)

    def rope_cols(t):
        return jnp.concatenate(
            [rope(t[:, c * LANES:(c + 1) * LANES]) for c in range(t.shape[1] // LANES)], axis=1)

    qa_ref[...] = rope_cols(h[:, 0:A_WIDTH]) * A_Q_SCALE
    ka_ref[...] = rope_cols(h[:, A_WIDTH:2 * A_WIDTH])
    va_ref[...] = h[:, 2 * A_WIDTH:3 * A_WIDTH]

    c0 = 3 * A_WIDTH
    cq = _rms(h[:, c0:c0 + Q_LORA], gq_ref[...])
    q = jnp.dot(cq.astype(BF16), w_uq_ref[...], preferred_element_type=F32)
    q_parts = []
    for hh in range(MLA_HEADS):
        base = hh * MLA_KPAD
        q_parts.append(q[:, base:base + QK_NOPE])
        q_parts.append(rope(q[:, base + QK_NOPE:base + MLA_KPAD]))
    qm_ref[...] = (jnp.concatenate(q_parts, axis=1) * MLA_Q_SCALE).astype(BF16)

    c1 = c0 + Q_LORA
    ckv = _rms(h[:, c1:c1 + KV_LORA], gkv_ref[...])
    kv = jnp.dot(ckv.astype(BF16), w_ukv_ref[...], preferred_element_type=F32)
    c2 = c1 + KV_LORA
    k_pe = rope(h[:, c2:c2 + LANES])
    k_parts = []
    for hh in range(MLA_HEADS):
        k_parts.append(kv[:, hh * QK_NOPE:(hh + 1) * QK_NOPE])
        k_parts.append(k_pe)
    km_ref[...] = jnp.concatenate(k_parts, axis=1).astype(BF16)
    vm_ref[...] = kv[:, MLA_HEADS * QK_NOPE:].astype(BF16)


def _mix_ffn_head(a_ref, b_ref, x_ref, wo_ref, ga_ref, gb_ref, g1_ref, beta1_ref,
                  w1_ref, w2_ref, g2_ref, beta2_ref):
    mixed = jnp.concatenate([_rms(a_ref[...], ga_ref[...]), _rms(b_ref[...], gb_ref[...])], axis=1)
    y = jnp.dot(mixed.astype(BF16), wo_ref[...], preferred_element_type=F32)
    x1 = _layer_norm(ALPHA * x_ref[...] + y, g1_ref[...], beta1_ref[...])
    xb = x1.astype(BF16)
    acc = ALPHA * x1
    for c in range(D_FF // FF_CHUNK):
        cols = slice(c * FF_CHUNK, (c + 1) * FF_CHUNK)
        hdn = jnp.dot(xb, w1_ref[:, cols], preferred_element_type=F32)
        hdn = jnp.square(jnp.maximum(hdn, 0.0)).astype(BF16)
        acc = acc + jnp.dot(hdn, w2_ref[cols, :], preferred_element_type=F32)
    return _layer_norm(acc, g2_ref[...], beta2_ref[...])


def _permute_rows(src, o_ref, slab_sc, *, inverse):
    tm, width = o_ref.shape
    tiles = TOKEN_GROUP // SUBLANES
    if not inverse:
        for c in range(width // LANES):
            slab_sc[c] = src[:, c * LANES:(c + 1) * LANES]
    for g in range(tm // TOKEN_GROUP):
        for tile in range(tiles):
            first_token = 4 * (tile & 3) + (tile >> 2)
            natural = pl.ds(g * TOKEN_GROUP + first_token, SUBLANES, stride=tiles)
            p0 = g * TOKEN_GROUP + tile * SUBLANES
            for c in range(width // LANES):
                if inverse:
                    slab_sc[c, natural, :] = src[p0:p0 + SUBLANES, c * LANES:(c + 1) * LANES]
                else:
                    o_ref[p0:p0 + SUBLANES, c * LANES:(c + 1) * LANES] = slab_sc[c, natural, :]
    if inverse:
        for c in range(width // LANES):
            o_ref[:, c * LANES:(c + 1) * LANES] = slab_sc[c]


N_MIX_INPUTS = 12
N_IN_PROJ_INPUTS = 7


def _stage_kernel(*refs, first, last):
    refs = list(refs)
    if first:
        head_refs, refs = refs[:1], refs[1:]
    else:
        head_refs, refs = refs[:N_MIX_INPUTS], refs[N_MIX_INPUTS:]
    if not last:
        tail_in, refs = refs[:N_IN_PROJ_INPUTS], refs[N_IN_PROJ_INPUTS:]
    x_out_ref, refs = refs[0], refs[1:]
    x_prev_sc = refs[-1]

    @pl.when(pl.program_id(0) == 0)
    def _():
        x_prev_sc[...] = jnp.zeros(x_prev_sc.shape, F32)

    x_prev = x_prev_sc[...]
    if first:
        _in_proj_tail(x_prev, *tail_in, *refs[:len(IN_PROJ_OUTS)])
        _permute_rows(head_refs[0], x_out_ref, refs[-2], inverse=False)
        x_prev_sc[...] = x_out_ref[...]
    elif last:
        _permute_rows(x_prev, x_out_ref, refs[-2], inverse=True)
        x_prev_sc[...] = _mix_ffn_head(*head_refs)
    else:
        _in_proj_tail(x_prev, *tail_in, *refs[:len(IN_PROJ_OUTS)])
        x_new = _mix_ffn_head(*head_refs)
        x_out_ref[...] = x_new
        x_prev_sc[...] = x_new


def _stage(head_args, tail_args, *, first, last, seq, tm=STAGE_TM):
    x = head_args[0] if first else head_args[2]
    t = x.shape[0]
    n = t // tm
    steps = n + 1
    row = lambda i: (jnp.minimum(i, n - 1), 0)
    tail_row = lambda i: (jnp.maximum(i - 1, 0), 0)
    n_pos_blocks = seq // tm
    pos = lambda i: (tail_row(i)[0] % n_pos_blocks, 0)
    if first:
        in_specs = [pl.BlockSpec((tm, D_MODEL), row)]
    else:
        in_specs = [pl.BlockSpec((tm, A_WIDTH), row), pl.BlockSpec((tm, MLA_WIDTH), row),
                    pl.BlockSpec((tm, D_MODEL), row)] + [_const_spec(c.shape) for c in head_args[3:]]
    out_specs = [pl.BlockSpec((tm, D_MODEL), tail_row if last else row)]
    out_shape = [jax.ShapeDtypeStruct((t, D_MODEL), F32)]
    if not last:
        in_specs += [_const_spec(c.shape) for c in tail_args[:5]]
        in_specs += [pl.BlockSpec((tm, LANES), pos), pl.BlockSpec((tm, LANES), pos)]
        out_specs += [pl.BlockSpec((tm, w), tail_row) for w, _ in IN_PROJ_OUTS]
        out_shape += [jax.ShapeDtypeStruct((t, w), dt) for w, dt in IN_PROJ_OUTS]
    scratch = [pltpu.VMEM((tm, D_MODEL), F32)]
    if first or last:
        scratch = [pltpu.VMEM((D_MODEL // LANES, tm, LANES), F32)] + scratch
    return pl.pallas_call(
        functools.partial(_stage_kernel, first=first, last=last),
        grid=(steps,),
        in_specs=in_specs, out_specs=out_specs, out_shape=out_shape, scratch_shapes=scratch,
        compiler_params=_params(1, STAGE_VMEM_LIMIT),
        name="stage_first" if first else ("stage_last" if last else "stage"),
    )(*head_args, *tail_args)


Q_TILES = 4


def _mla_kernel(q_ref, k_ref, v_ref, o_ref, m_sc, acc_sc, *, tq):
    first_diag = Q_TILES * pl.program_id(1)
    tk = tq
    m_sc[...] = jnp.full(m_sc.shape, NEG_INF, F32)
    acc_sc[...] = jnp.zeros(acc_sc.shape, F32)

    def tile(qt, k_tile, causal, q_lo=0, k_lo=0, k_n=tk):
        k_start = pl.multiple_of(k_tile * tk, tk) + k_lo
        rows = slice(q_lo, tq)
        for h in range(MLA_HEADS):
            q = q_ref[qt * tq + q_lo:(qt + 1) * tq, h * MLA_KPAD:(h + 1) * MLA_KPAD]
            k = k_ref[pl.ds(k_start, k_n), h * MLA_KPAD:(h + 1) * MLA_KPAD]
            v = v_ref[pl.ds(k_start, k_n), h * V_DIM:(h + 1) * V_DIM]
            s = lax.dot_general(q, k, (((1,), (1,)), ((), ())), preferred_element_type=F32)
            if causal is not None:
                s = jnp.where(causal, s, NEG_INF)
            m_prev = m_sc[qt, h, rows]
            m_new = jnp.maximum(m_prev, jnp.max(s, -1, keepdims=True))
            alpha = jnp.exp2(m_prev - m_new)
            p = jnp.exp2(s - jnp.concatenate([m_new] * (k_n // LANES), axis=1))
            v1 = jnp.concatenate([v, jnp.ones_like(v)], axis=1)
            acc_sc[qt, h, rows] = (jnp.concatenate([alpha, alpha], axis=1) * acc_sc[qt, h, rows]
                                   + jnp.dot(p.astype(BF16), v1, preferred_element_type=F32))
            m_sc[qt, h, rows] = m_new

    def below_diagonal(k_tile, carry):
        for qt in range(Q_TILES):
            tile(qt, k_tile, None)
        return carry

    lax.fori_loop(0, first_diag, below_diagonal, 0)

    r = lax.broadcasted_iota(jnp.int32, (tq, tk), 0)
    c = lax.broadcasted_iota(jnp.int32, (tq, tk), 1)
    causal = ((c & -TOKEN_GROUP) + _position_in_group(c)) <= ((r & -TOKEN_GROUP) + _position_in_group(r))
    half = tq // 2
    for qt in range(Q_TILES):
        for kt in range(qt):
            tile(qt, first_diag + kt, None)
        tile(qt, first_diag + qt, causal[:, :half], k_n=half)
        tile(qt, first_diag + qt, causal[half:, half:], q_lo=half, k_lo=half, k_n=half)
    for qt in range(Q_TILES):
        for h in range(MLA_HEADS):
            acc = acc_sc[qt, h]
            o_ref[qt * tq:(qt + 1) * tq, h * V_DIM:(h + 1) * V_DIM] = acc[:, :V_DIM] / acc[:, V_DIM:]


def _mla_attention(qm, km, vm, *, batch, seq, tq):
    t = qm.shape[0]
    steps = seq // (Q_TILES * tq)
    q_map = lambda b, i: (b * steps + i, 0)
    kv_map = lambda b, i: (b, 0)
    return pl.pallas_call(
        functools.partial(_mla_kernel, tq=tq),
        grid=(batch, steps),
        in_specs=[
            pl.BlockSpec((Q_TILES * tq, MLA_HEADS * MLA_KPAD), q_map),
            pl.BlockSpec((seq, MLA_HEADS * MLA_KPAD), kv_map),
            pl.BlockSpec((seq, MLA_WIDTH), kv_map),
        ],
        out_specs=pl.BlockSpec((Q_TILES * tq, MLA_WIDTH), q_map),
        out_shape=jax.ShapeDtypeStruct((t, MLA_WIDTH), F32),
        scratch_shapes=[pltpu.VMEM((Q_TILES, MLA_HEADS, tq, LANES), F32),
                        pltpu.VMEM((Q_TILES, MLA_HEADS, tq, 2 * V_DIM), F32)],
        compiler_params=_params(2),
        name="mla_attention",
    )(qm, km, vm)


HEAD_PAIRS = A_WIDTH // LANES


def _sub_index(i, dilation):
    if dilation == 16:
        return i
    if dilation == 4:
        return 32 * (i >> 5) + 4 * (i & 7) + ((i >> 3) & 3)
    return _position_in_group(i)


def _band_bias(dilation, has_prev):
    i = lax.broadcasted_iota(jnp.int32, (2 * SPAN, 2 * SPAN), 0) & (SPAN - 1)
    j = lax.broadcasted_iota(jnp.int32, (2 * SPAN, 2 * SPAN), 1)
    dist = SPAN + _sub_index(i, dilation) - ((j & SPAN) + _sub_index(j & (SPAN - 1), dilation))
    valid = (dist >= 0) & (dist <= SPAN)
    if has_prev is not None:
        valid = valid & ((j >= SPAN) | has_prev)
    return jnp.where(valid, 0.0, NEG_INF)


def _dilated_kernel(q_ref, k_ref, v_ref, o_ref, kp_sc, vp_sc, m_sc, l_sc):
    @pl.when((pl.program_id(0) == 0) & (pl.program_id(1) == 0))
    def _():
        kp_sc[...] = jnp.zeros(kp_sc.shape, F32)
        vp_sc[...] = jnp.zeros(vp_sc.shape, F32)

    has_prev = pl.program_id(1) > 0
    lane = lax.broadcasted_iota(jnp.int32, (SPAN, LANES), 1)
    head_a = lane < A_HEAD_DIM
    ones = jnp.ones((2 * SPAN, LANES), BF16)

    def rows2d(t):
        return t.reshape(-1, LANES)

    def attend(q, k, v, bias, idx, first, last):
        qa = jnp.where(head_a, q, 0.0).astype(BF16)
        qb = jnp.where(head_a, 0.0, q).astype(BF16)
        q2 = jnp.concatenate([qa, qb], axis=0)
        s = lax.dot_general(q2, k.astype(BF16), (((1,), (1,)), ((), ())), preferred_element_type=F32) + bias
        m = jnp.max(s, -1, keepdims=True)
        p = jnp.exp2(s - m)
        pv = jnp.dot(p.astype(BF16), jnp.concatenate([v.astype(BF16), ones], axis=1),
                     preferred_element_type=F32)
        acc = jnp.where(head_a, pv[:SPAN, :LANES], pv[SPAN:, :LANES])
        l = jnp.where(head_a, pv[:SPAN, LANES:], pv[SPAN:, LANES:])
        m_cur = jnp.where(head_a, jnp.broadcast_to(m[:SPAN], (SPAN, LANES)),
                          jnp.broadcast_to(m[SPAN:], (SPAN, LANES)))
        shape = o_ref[idx].shape
        if first:
            o_ref[idx] = acc.reshape(shape)
            l_sc[idx] = l.reshape(shape)
            m_sc[idx] = m_cur.reshape(shape)
            return
        m_old = rows2d(m_sc[idx])
        m_new = jnp.maximum(m_old, m_cur)
        a_old = jnp.exp2(m_old - m_new)
        a_cur = jnp.exp2(m_cur - m_new)
        acc = a_old * rows2d(o_ref[idx]) + a_cur * acc
        l = a_old * rows2d(l_sc[idx]) + a_cur * l
        if last:
            o_ref[idx] = (acc / l).reshape(shape)
        else:
            o_ref[idx] = acc.reshape(shape)
            l_sc[idx] = l.reshape(shape)
            m_sc[idx] = m_new.reshape(shape)

    every = slice(None)

    band = _band_bias(1, None)
    band_first = _band_bias(1, has_prev)
    for p in range(HEAD_PAIRS):
        cols = pl.ds(p * LANES, LANES)
        k = jnp.concatenate([rows2d(kp_sc[CHUNK_GROUPS - 1, :, :, cols]), rows2d(k_ref[0, :, :, cols])], axis=0)
        v = jnp.concatenate([rows2d(vp_sc[CHUNK_GROUPS - 1, :, :, cols]), rows2d(v_ref[0, :, :, cols])], axis=0)
        attend(rows2d(q_ref[0, :, :, cols]), k, v, band_first, (0, every, every, cols), True, False)

    def d1_body(g, carry):
        for p in range(HEAD_PAIRS):
            cols = pl.ds(p * LANES, LANES)
            attend(rows2d(q_ref[g, :, :, cols]), rows2d(k_ref[pl.ds(g - 1, 2), :, :, cols]),
                   rows2d(v_ref[pl.ds(g - 1, 2), :, :, cols]), band, (g, every, every, cols), True, False)
        return carry
    lax.fori_loop(1, CHUNK_GROUPS, d1_body, 0, unroll=True)

    band = _band_bias(4, None)
    band_first = _band_bias(4, has_prev)

    def d4_first(r4, carry):
        res = pl.ds(4 * r4, 4)
        for p in range(HEAD_PAIRS):
            cols = pl.ds(p * LANES, LANES)
            k = jnp.concatenate([rows2d(kp_sc[CHUNK_GROUPS - 4:, res, :, cols]), rows2d(k_ref[0:4, res, :, cols])],
                                axis=0)
            v = jnp.concatenate([rows2d(vp_sc[CHUNK_GROUPS - 4:, res, :, cols]), rows2d(v_ref[0:4, res, :, cols])],
                                axis=0)
            attend(rows2d(q_ref[0:4, res, :, cols]), k, v, band_first, (slice(0, 4), res, every, cols), False, False)
        return carry
    lax.fori_loop(0, 4, d4_first, 0, unroll=True)

    blocks_after_first = CHUNK_GROUPS // 4 - 1

    def d4_body(t, carry):
        res = pl.ds(4 * (t // blocks_after_first), 4)
        g0 = 4 * (1 + t % blocks_after_first)
        for p in range(HEAD_PAIRS):
            cols = pl.ds(p * LANES, LANES)
            attend(rows2d(q_ref[pl.ds(g0, 4), res, :, cols]), rows2d(k_ref[pl.ds(g0 - 4, 8), res, :, cols]),
                   rows2d(v_ref[pl.ds(g0 - 4, 8), res, :, cols]), band, (pl.ds(g0, 4), res, every, cols),
                   False, False)
        return carry
    lax.fori_loop(0, 4 * blocks_after_first, d4_body, 0, unroll=True)

    band16 = _band_bias(16, has_prev)

    def d16_body(r, carry):
        for p in range(HEAD_PAIRS):
            cols = pl.ds(p * LANES, LANES)
            k = jnp.concatenate([rows2d(kp_sc[:, r, :, cols]), rows2d(k_ref[:, r, :, cols])], axis=0)
            v = jnp.concatenate([rows2d(vp_sc[:, r, :, cols]), rows2d(v_ref[:, r, :, cols])], axis=0)
            attend(rows2d(q_ref[:, r, :, cols]), k, v, band16, (every, r, every, cols), False, True)
        return carry
    lax.fori_loop(0, MAX_DILATION, d16_body, 0, unroll=True)

    kp_sc[...] = k_ref[...]
    vp_sc[...] = v_ref[...]


def _dilated_attention(qa, ka, va, *, batch, seq):
    t = qa.shape[0]
    chunks = seq // (CHUNK_GROUPS * TOKEN_GROUP)
    tiles = TOKEN_GROUP // SUBLANES
    shape4 = (t // TOKEN_GROUP, tiles, SUBLANES, A_WIDTH)
    view = lambda a: a.reshape(shape4)
    block = (CHUNK_GROUPS, tiles, SUBLANES, A_WIDTH)
    spec = pl.BlockSpec(block, lambda b, c: (b * chunks + c, 0, 0, 0))
    scratch = pltpu.VMEM(block, F32)
    out = pl.pallas_call(
        _dilated_kernel,
        grid=(batch, chunks),
        in_specs=[spec, spec, spec],
        out_specs=spec,
        out_shape=jax.ShapeDtypeStruct(shape4, F32),
        scratch_shapes=[scratch, scratch, scratch, scratch],
        compiler_params=_params(2),
        name="dilated",
    )(view(qa), view(ka), view(va))
    return out.reshape(t, A_WIDTH)


def _rope_tables(seq):
    half = A_HEAD_DIM // 2
    inv_freq = ROPE_THETA ** (-jnp.arange(half, dtype=F32) / half)
    row = jnp.arange(seq, dtype=jnp.int32)
    pos = (row & -TOKEN_GROUP) + _position_in_group(row)
    ang = pos.astype(F32)[:, None] * inv_freq[None, :]
    cos = jnp.cos(ang)
    sin = jnp.sin(ang)
    cos128 = jnp.tile(cos, (1, LANES // half))
    sin128 = jnp.tile(jnp.concatenate([-sin, sin], axis=1), (1, LANES // A_HEAD_DIM))
    return cos128, sin128


def _prep_w_uq(w):
    w = w.reshape(Q_LORA, MLA_HEADS, QK_NOPE + QK_ROPE)
    w = jnp.pad(w, ((0, 0), (0, 0), (0, MLA_KPAD - QK_NOPE - QK_ROPE)))
    return w.reshape(Q_LORA, MLA_HEADS * MLA_KPAD).astype(BF16)


def _prep_w_ukv(w):
    w = w.reshape(KV_LORA, MLA_HEADS, QK_NOPE + V_DIM)
    k = w[:, :, :QK_NOPE].reshape(KV_LORA, MLA_HEADS * QK_NOPE)
    v = w[:, :, QK_NOPE:].reshape(KV_LORA, MLA_HEADS * V_DIM)
    return jnp.concatenate([k, v], axis=1).astype(BF16)


def kernel(x, w_in, q_a_norm, kv_a_norm, w_uq, w_ukv, a_out_norm, b_out_norm, w_o,
           ln1_g, ln1_b, w_ff1, w_ff2, ln2_g, ln2_b):
    batch, seq, _ = x.shape
    assert seq % (CHUNK_GROUPS * TOKEN_GROUP) == 0
    t = batch * seq
    cos, sin = _rope_tables(seq)
    row2d = lambda v: v.reshape(1, -1)

    def in_proj_args(l):
        w_in_l = jnp.pad(w_in[l], ((0, 0), (0, IN_COLS_PAD - w_in.shape[2]))).astype(BF16)
        return (w_in_l, _prep_w_uq(w_uq[l]), _prep_w_ukv(w_ukv[l]), row2d(q_a_norm[l]), row2d(kv_a_norm[l]),
                cos, sin)

    def mix_args(l):
        return (w_o[l].astype(BF16), row2d(a_out_norm[l]), row2d(b_out_norm[l]), row2d(ln1_g[l]),
                row2d(ln1_b[l]), w_ff1[l].astype(BF16), w_ff2[l].astype(BF16), row2d(ln2_g[l]), row2d(ln2_b[l]))

    xf, qa, ka, va, qm, km, vm = _stage((x.reshape(t, D_MODEL),), in_proj_args(0),
                                        first=True, last=False, seq=seq)
    for l in range(DEPTH):
        b_out = _mla_attention(qm, km, vm, batch=batch, seq=seq, tq=512)
        a_out = _dilated_attention(qa, ka, va, batch=batch, seq=seq)
        head = (a_out, b_out, xf) + mix_args(l)
        if l + 1 < DEPTH:
            xf, qa, ka, va, qm, km, vm = _stage(head, in_proj_args(l + 1), first=False, last=False, seq=seq)
        else:
            (xf,) = _stage(head, (), first=False, last=True, seq=seq)
    return xf.reshape(batch, seq, D_MODEL)
```
